```python
import numpy as np
import jax
import jax.numpy as jnp
from jax import lax

D_MODEL = 4096
BATCH = 4
SEQ = 4096
DEPTH = 2

N_SUBLAYERS = 3
N_MOD = 3 * N_SUBLAYERS
D_FF = 2 * D_MODEL
FFN_RES = 0.5
RMS_EPS = 1e-6

LRU_WIDTH = D_MODEL // 2
LRU_BLOCKS = 16
CONV_WIDTH = 4
LRU_C = 8.0

N_HEADS = 16
N_KV_HEADS = 4
HEAD_DIM = 128
ATTN_WIDTH = N_HEADS * HEAD_DIM
KV_WIDTH = N_KV_HEADS * HEAD_DIM
KV_GROUP = N_HEADS // N_KV_HEADS
ROT_DIM = HEAD_DIM // 4
IDX_HEADS = 16
IDX_DIM = 64
IDX_ROT_DIM = IDX_DIM // 4
TOPK_MAX = 256
Q_BLOCK = 128
ROPE_THETA = 500000.0

POOL_WINDOWS = (2, 4, 8, 16)
POOL_WIDTH = D_MODEL // 2
POOL_GROUP = POOL_WIDTH // 4

N_BRANCH = 3
SPLIT_SIZES = (LRU_WIDTH, LRU_WIDTH, ATTN_WIDTH, KV_WIDTH, KV_WIDTH,
               IDX_HEADS * IDX_DIM, IDX_DIM, IDX_HEADS, POOL_WIDTH, N_BRANCH * D_MODEL)
D_IN = sum(SPLIT_SIZES)

kernel_name = 'hybrid_rglru_dsa_pool_macaron_block'


def rmsnorm(x, gain):
    xf = x.astype(jnp.float32)
    xf = xf * lax.rsqrt(jnp.mean(xf * xf, axis=-1, keepdims=True) + RMS_EPS)
    return xf.astype(x.dtype) * gain


def swiglu(h, w_gate, w_up, w_down):
    return (jax.nn.silu(h @ w_gate) * (h @ w_up)) @ w_down


def rope_partial(t, positions, rot_dim):
    half = rot_dim // 2
    inv_freq = ROPE_THETA ** (-jnp.arange(half, dtype=jnp.float32) / half)
    ang = positions.astype(jnp.float32)[..., None] * inv_freq
    cos = jnp.cos(ang)[:, :, None, :]
    sin = jnp.sin(ang)[:, :, None, :]
    x1 = t[..., :half].astype(jnp.float32)
    x2 = t[..., half:rot_dim].astype(jnp.float32)
    rot = jnp.concatenate([x1 * cos - x2 * sin, x2 * cos + x1 * sin], axis=-1)
    return jnp.concatenate([rot.astype(t.dtype), t[..., rot_dim:]], axis=-1)


def causal_depthwise_conv(u, w, b):
    out = lax.conv_general_dilated(
        u, w[:, None, :], window_strides=(1,), padding=[(CONV_WIDTH - 1, 0)],
        dimension_numbers=('NWC', 'WIO', 'NWC'), feature_group_count=u.shape[-1])
    return out + b


def _lru_combine(e1, e2):
    a1, b1 = e1
    a2, b2 = e2
    return a1 * a2, a2 * b1 + b2


def rglru_branch(u_x, u_gate, conv_w, conv_b, w_a, b_a, w_x, b_x, lam):
    B, S, C = u_x.shape
    xc = causal_depthwise_conv(u_x, conv_w, conv_b)
    xb = xc.reshape(B, S, LRU_BLOCKS, C // LRU_BLOCKS)
    r = jax.nn.sigmoid(jnp.einsum('bshi,hij->bshj', xb, w_a).reshape(B, S, C) + b_a)
    i = jax.nn.sigmoid(jnp.einsum('bshi,hij->bshj', xb, w_x).reshape(B, S, C) + b_x)
    log_a = -LRU_C * r.astype(jnp.float32) * jax.nn.softplus(-lam.astype(jnp.float32))
    a = jnp.exp(log_a)
    b = jnp.sqrt(-jnp.expm1(2.0 * log_a)) * (i * xc).astype(jnp.float32)
    _, h = lax.associative_scan(_lru_combine, (a, b), axis=1)
    return h.astype(u_x.dtype) * jax.nn.gelu(u_gate)


def sparse_attention_branch(q, k, v, q_idx, k_idx, w_idx, positions):
    B, S, _ = q.shape
    topk = min(TOPK_MAX, S // 4)
    nb = S // Q_BLOCK
    q = rope_partial(q.reshape(B, S, N_HEADS, HEAD_DIM), positions, ROT_DIM)
    k = rope_partial(k.reshape(B, S, N_KV_HEADS, HEAD_DIM), positions, ROT_DIM)
    v = v.reshape(B, S, N_KV_HEADS, HEAD_DIM)
    q_idx = rope_partial(q_idx.reshape(B, S, IDX_HEADS, IDX_DIM), positions, IDX_ROT_DIM)
    k_idx = rope_partial(k_idx.reshape(B, S, 1, IDX_DIM), positions, IDX_ROT_DIM)[:, :, 0]
    k_idx_f = k_idx.astype(jnp.float32)
    w_idx = w_idx.astype(jnp.float32) * (IDX_HEADS * IDX_DIM) ** -0.5
    key_pos = jnp.arange(S)

    def to_blocks(t):
        return jnp.moveaxis(t.reshape(B, nb, Q_BLOCK, *t.shape[2:]), 1, 0)

    def block_fn(args):
        qb, qib, wb, tb = args
        s_h = jax.nn.relu(jnp.einsum('bqhd,bsd->bhqs', qib.astype(jnp.float32), k_idx_f))
        score = jnp.einsum('bhqs,bqh->bqs', s_h, wb)
        causal = key_pos[None, :] <= tb[:, None]
        score = jnp.where(causal[None], score, -jnp.inf)
        _, idx = lax.top_k(score, topk)
        valid = idx <= tb[None, :, None]
        kg = jax.vmap(lambda kk, ii: kk[ii])(k, idx)
        vg = jax.vmap(lambda vv, ii: vv[ii])(v, idx)
        qr = qb.reshape(B, Q_BLOCK, N_KV_HEADS, KV_GROUP, HEAD_DIM).astype(jnp.float32)
        logits = jnp.einsum('bqgrd,bqkgd->bqgrk', qr, kg.astype(jnp.float32)) * HEAD_DIM ** -0.5
        logits = jnp.where(valid[:, :, None, None, :], logits, -jnp.inf)
        p = jax.nn.softmax(logits, axis=-1)
        o = jnp.einsum('bqgrk,bqkgd->bqgrd', p, vg.astype(jnp.float32))
        return o.reshape(B, Q_BLOCK, ATTN_WIDTH).astype(qb.dtype)

    out = lax.map(block_fn, (to_blocks(q), to_blocks(q_idx), to_blocks(w_idx),
                             key_pos.reshape(nb, Q_BLOCK)))
    return jnp.moveaxis(out, 0, 1).reshape(B, S, ATTN_WIDTH)


def pool_branch(u, pool_w, pool_scale):
    B, S, _ = u.shape
    ug = u.reshape(B, S, len(POOL_WINDOWS), POOL_GROUP)
    ugf = ug.astype(jnp.float32)
    cs = jnp.cumsum(ugf, axis=1)
    pooled = []
    for gi, w in enumerate(POOL_WINDOWS):
        csg = cs[:, :, gi]
        prev = jnp.pad(csg, ((0, 0), (w, 0), (0, 0)))[:, :S]
        cnt = jnp.minimum(jnp.arange(1, S + 1), w).astype(jnp.float32)[None, :, None]
        pooled.append((csg - prev) / cnt)
    pooled = jnp.stack(pooled, axis=2) - ugf
    mixed = jnp.einsum('bsgc,gcd->bsgd', pooled.astype(u.dtype), pool_w)
    return mixed.reshape(B, S, POOL_WIDTH) * pool_scale


def hybrid_mixer(h, positions, w_in, conv_w, conv_b, lru_w_a, lru_b_a, lru_w_x, lru_b_x,
                 lru_lambda, pool_w, pool_scale, w_br_lru, w_br_attn, w_br_pool, w_out):
    B, S, D = h.shape
    proj = h @ w_in
    offsets = np.cumsum(SPLIT_SIZES)[:-1].tolist()
    (u_lru, u_gate, q, k, v, q_idx, k_idx, w_idx, u_pool, g_br) = jnp.split(proj, offsets, axis=-1)
    y_lru = rglru_branch(u_lru, u_gate, conv_w, conv_b, lru_w_a, lru_b_a, lru_w_x, lru_b_x,
                         lru_lambda) @ w_br_lru
    y_attn = sparse_attention_branch(q, k, v, q_idx, k_idx, w_idx, positions) @ w_br_attn
    y_pool = pool_branch(u_pool, pool_w, pool_scale) @ w_br_pool
    g = jax.nn.sigmoid(g_br.astype(jnp.float32)).reshape(B, S, N_BRANCH, D).astype(h.dtype)
    merged = g[:, :, 0] * y_lru + g[:, :, 1] * y_attn + g[:, :, 2] * y_pool
    return merged @ w_out


def setup_inputs(seed: int = 0) -> dict:
    key = jax.random.key(seed)
    ks = jax.random.split(key, 32)
    f32 = jnp.float32

    def nrm(k, shape, fan_in):
        return jax.random.normal(k, shape, f32) * fan_in ** -0.5

    bw = LRU_WIDTH // LRU_BLOCKS
    a_pow = jax.random.uniform(ks[18], (DEPTH, LRU_WIDTH), f32, 0.9, 0.999)
    a_base = a_pow ** (1.0 / LRU_C)
    lru_lambda = jnp.log(a_base) - jnp.log1p(-a_base)
    positions = (jnp.arange(SEQ, dtype=jnp.int32)[None, :]
                 + jax.random.randint(ks[2], (BATCH, 1), 0, 1024, dtype=jnp.int32))
    return {
        'x': jax.random.normal(ks[0], (BATCH, SEQ, D_MODEL), f32),
        'c': jax.random.normal(ks[1], (BATCH, D_MODEL), f32),
        'positions': positions,
        'w_mod': nrm(ks[3], (D_MODEL, N_MOD * D_MODEL), D_MODEL),
        'b_mod': 0.01 * jax.random.normal(ks[4], (N_MOD * D_MODEL,), f32),
        'mod_offset': 0.1 * jax.random.normal(ks[5], (DEPTH, N_MOD, D_MODEL), f32),
        'norm_pre': 1.0 + 0.05 * jax.random.normal(ks[6], (DEPTH, N_SUBLAYERS, D_MODEL), f32),
        'norm_post': 1.0 + 0.05 * jax.random.normal(ks[7], (DEPTH, N_SUBLAYERS, D_MODEL), f32),
        'ffn1_w_gate': nrm(ks[8], (DEPTH, D_MODEL, D_FF), D_MODEL),
        'ffn1_w_up': nrm(ks[9], (DEPTH, D_MODEL, D_FF), D_MODEL),
        'ffn1_w_down': nrm(ks[10], (DEPTH, D_FF, D_MODEL), D_FF),
        'w_in': nrm(ks[11], (DEPTH, D_MODEL, D_IN), D_MODEL),
        'conv_w': nrm(ks[12], (DEPTH, CONV_WIDTH, LRU_WIDTH), CONV_WIDTH),
        'conv_b': 0.01 * jax.random.normal(ks[13], (DEPTH, LRU_WIDTH), f32),
        'lru_w_a': nrm(ks[14], (DEPTH, LRU_BLOCKS, bw, bw), bw),
        'lru_b_a': 0.01 * jax.random.normal(ks[15], (DEPTH, LRU_WIDTH), f32),
        'lru_w_x': nrm(ks[16], (DEPTH, LRU_BLOCKS, bw, bw), bw),
        'lru_b_x': 0.01 * jax.random.normal(ks[17], (DEPTH, LRU_WIDTH), f32),
        'lru_lambda': lru_lambda,
        'pool_w': nrm(ks[19], (DEPTH, len(POOL_WINDOWS), POOL_GROUP, POOL_GROUP), POOL_GROUP),
        'pool_scale': 1.0 + 0.1 * jax.random.normal(ks[20], (DEPTH, POOL_WIDTH), f32),
        'w_br_lru': nrm(ks[21], (DEPTH, LRU_WIDTH, D_MODEL), LRU_WIDTH),
        'w_br_attn': nrm(ks[22], (DEPTH, ATTN_WIDTH, D_MODEL), ATTN_WIDTH),
        'w_br_pool': nrm(ks[23], (DEPTH, POOL_WIDTH, D_MODEL), POOL_WIDTH),
        'w_out': nrm(ks[24], (DEPTH, D_MODEL, D_MODEL), D_MODEL),
        'ffn2_w_gate': nrm(ks[25], (DEPTH, D_MODEL, D_FF), D_MODEL),
        'ffn2_w_up': nrm(ks[26], (DEPTH, D_MODEL, D_FF), D_MODEL),
        'ffn2_w_down': nrm(ks[27], (DEPTH, D_FF, D_MODEL), D_FF),
    }


def reference(x, c, positions, w_mod, b_mod, mod_offset, norm_pre, norm_post,
              ffn1_w_gate, ffn1_w_up, ffn1_w_down, w_in, conv_w, conv_b,
              lru_w_a, lru_b_a, lru_w_x, lru_b_x, lru_lambda, pool_w, pool_scale,
              w_br_lru, w_br_attn, w_br_pool, w_out, ffn2_w_gate, ffn2_w_up, ffn2_w_down):
    B = x.shape[0]
    mod = (jax.nn.silu(c) @ w_mod + b_mod).reshape(B, N_MOD, D_MODEL)

    for l in range(DEPTH):
        m = mod + mod_offset[l]

        def pre(h, j):
            hn = rmsnorm(h, norm_pre[l, j])
            return hn * (1.0 + m[:, 3 * j + 1][:, None, :]) + m[:, 3 * j][:, None, :]

        def gate(j):
            return m[:, 3 * j + 2][:, None, :]

        y = swiglu(pre(x, 0), ffn1_w_gate[l], ffn1_w_up[l], ffn1_w_down[l])
        x = x + FFN_RES * gate(0) * rmsnorm(y, norm_post[l, 0])
        y = hybrid_mixer(pre(x, 1), positions, w_in[l], conv_w[l], conv_b[l], lru_w_a[l],
                         lru_b_a[l], lru_w_x[l], lru_b_x[l], lru_lambda[l], pool_w[l],
                         pool_scale[l], w_br_lru[l], w_br_attn[l], w_br_pool[l], w_out[l])
        x = x + gate(1) * rmsnorm(y, norm_post[l, 1])
        y = swiglu(pre(x, 2), ffn2_w_gate[l], ffn2_w_up[l], ffn2_w_down[l])
        x = x + FFN_RES * gate(2) * rmsnorm(y, norm_post[l, 2])
    return x
```

```python
import functools

import numpy as np
import jax
import jax.numpy as jnp
from jax import lax
from jax.experimental import pallas as pl
from jax.experimental.pallas import tpu as pltpu

F32 = jnp.float32
BF16 = jnp.bfloat16

N_MOD = 9
FFN_RES = 0.5
RMS_EPS = 1e-6
CONV_WIDTH = 4
LRU_C = 8.0
HEAD_DIM = 128
N_KV_HEADS = 4
ROT_DIM = HEAD_DIM // 4
IDX_HEADS = 16
IDX_DIM = 64
IDX_ROT_DIM = IDX_DIM // 4
TOPK_MAX = 256
ROPE_THETA = 500000.0
POOL_WINDOWS = (2, 4, 8, 16)

LANES = 128
MOD_ROWS = 8
MIB = 1024 * 1024
INT_MIN = -2 ** 31
MASK_NEG = -1e30


def _cparams(semantics, vmem_mib):
    return pltpu.CompilerParams(dimension_semantics=semantics, vmem_limit_bytes=vmem_mib * MIB)


def _rms(x):
    return x * lax.rsqrt(jnp.mean(x * x, axis=-1, keepdims=True) + RMS_EPS)


def _prenorm(x, m, gain, sub):
    shift = m[3 * sub:3 * sub + 1]
    scale = m[3 * sub + 1:3 * sub + 2]
    return _rms(x) * gain * (1.0 + scale) + shift


def _dot(a, b):
    return jnp.dot(a, b, preferred_element_type=F32)


def _dot_nt(a, b):
    return lax.dot_general(a, b, (((1,), (1,)), ((), ())), preferred_element_type=F32)


def _mod_kernel(c_ref, w_ref, b_ref, off_ref, o_ref):
    c = c_ref[...]
    s = (c * jax.nn.sigmoid(c)).astype(BF16)
    base = _dot(s, w_ref[...].astype(BF16)) + b_ref[...]
    for l in range(o_ref.shape[0]):
        o_ref[l] = base + off_ref[l:l + 1, :]


def _modulation(c, w_mod, b_mod, mod_offset):
    B, D = c.shape
    depth = mod_offset.shape[0]
    n = w_mod.shape[1]
    tn = min(1024, n)
    c_pad = jnp.zeros((MOD_ROWS, D), F32).at[:B].set(c)
    out = pl.pallas_call(
        _mod_kernel,
        grid=(n // tn,),
        in_specs=[
            pl.BlockSpec((MOD_ROWS, D), lambda j: (0, 0)),
            pl.BlockSpec((D, tn), lambda j: (0, j)),
            pl.BlockSpec((1, tn), lambda j: (0, j)),
            pl.BlockSpec((depth, tn), lambda j: (0, j)),
        ],
        out_specs=pl.BlockSpec((depth, MOD_ROWS, tn), lambda j: (0, 0, j)),
        out_shape=jax.ShapeDtypeStruct((depth, MOD_ROWS, n), F32),
        compiler_params=_cparams(("arbitrary",), 48),
        name="modulation",
    )(c_pad, w_mod, b_mod.reshape(1, n), mod_offset.reshape(depth, n))
    return out.reshape(depth, MOD_ROWS, N_MOD, D)


def _tables_kernel(pos_ref, pat_ref, o_ref):
    pos = pos_ref[...]
    for t in range(3):
        ang = pos * pat_ref[t, 0:1, :]
        cos = jnp.cos(ang)
        sin = jnp.sin(ang)
        o_ref[t, 0] = cos * pat_ref[t, 1:2, :] + pat_ref[t, 2:3, :]
        o_ref[t, 1] = sin * pat_ref[t, 3:4, :]
        o_ref[t, 2] = sin * pat_ref[t, 4:5, :]


def _rope_pattern(head_dim, rot_dim, n_lanes):
    half = rot_dim // 2
    inv_freq = ROPE_THETA ** (-jnp.arange(half, dtype=F32) / half)
    d = np.arange(n_lanes) % head_dim
    in_rot = d < rot_dim
    invf = jnp.where(in_rot, inv_freq[d % half], 0.0)
    rows = [invf,
            jnp.asarray(in_rot, F32),
            jnp.asarray(~in_rot, F32),
            jnp.asarray(-(d < half).astype(np.float32)),
            jnp.asarray(((d >= half) & in_rot).astype(np.float32))]
    return jnp.stack(rows + [jnp.zeros((n_lanes,), F32)] * 3)


def _rope_tables(positions):
    M = positions.size
    attn = _rope_pattern(HEAD_DIM, ROT_DIM, LANES)
    idx = _rope_pattern(IDX_DIM, IDX_ROT_DIM, LANES)
    lane = np.arange(LANES)
    is_k = jnp.asarray(lane < IDX_DIM, F32)
    is_w = (lane >= IDX_DIM) & (lane < IDX_DIM + IDX_HEADS)
    w_scale = float(IDX_HEADS * IDX_DIM) ** -0.5
    kw = idx * is_k[None, :]
    kw = kw.at[2].set(idx[2] * is_k + jnp.asarray(is_w, F32) * w_scale)
    pat = jnp.stack([attn, idx, kw])
    tm = min(512, M)
    return pl.pallas_call(
        _tables_kernel,
        grid=(M // tm,),
        in_specs=[pl.BlockSpec((tm, 1), lambda i: (i, 0)),
                  pl.BlockSpec((3, 8, LANES), lambda i: (0, 0, 0))],
        out_specs=pl.BlockSpec((3, 3, tm, LANES), lambda i: (0, 0, i, 0)),
        out_shape=jax.ShapeDtypeStruct((3, 3, M, LANES), F32),
        compiler_params=_cparams(("parallel",), 32),
        name="rope_tables",
    )(positions.reshape(M, 1).astype(F32), pat)


def _ffn_kernel(x_ref, m_ref, gpre_ref, gpost_ref, wg_ref, wu_ref, wd_ref, o_ref, xn_ref, *, sub):
    j = pl.program_id(1)

    @pl.when(j == 0)
    def _():
        xn_ref[...] = _prenorm(x_ref[...], m_ref[...], gpre_ref[...], sub).astype(BF16)
        o_ref[...] = jnp.zeros_like(o_ref)

    xn = xn_ref[...]
    h = _dot(xn, wg_ref[...])
    u = _dot(xn, wu_ref[...])
    a = (h * jax.nn.sigmoid(h) * u).astype(BF16)
    o_ref[...] += _dot(a, wd_ref[...])

    @pl.when(j == pl.num_programs(1) - 1)
    def _():
        gate = m_ref[3 * sub + 2:3 * sub + 3, :]
        o_ref[...] = x_ref[...] + (FFN_RES * gate) * (_rms(o_ref[...]) * gpost_ref[...])


def _ffn(x, mods, l, sub, gpre, gpost, wg, wu, wd, S):
    M, D = x.shape
    F = wg.shape[-1]
    tm = min(512, S)
    tf = min(256, F)
    tpb = S // tm
    return pl.pallas_call(
        functools.partial(_ffn_kernel, sub=sub),
        grid=(M // tm, F // tf),
        in_specs=[
            pl.BlockSpec((tm, D), lambda i, j: (i, 0), pipeline_mode=pl.Buffered(1)),
            pl.BlockSpec((None, None, N_MOD, D), lambda i, j: (l, i // tpb, 0, 0)),
            pl.BlockSpec((1, D), lambda i, j: (0, 0)),
            pl.BlockSpec((1, D), lambda i, j: (0, 0)),
            pl.BlockSpec((None, D, tf), lambda i, j: (l, 0, j)),
            pl.BlockSpec((None, D, tf), lambda i, j: (l, 0, j)),
            pl.BlockSpec((None, tf, D), lambda i, j: (l, j, 0)),
        ],
        out_specs=pl.BlockSpec((tm, D), lambda i, j: (i, 0)),
        out_shape=jax.ShapeDtypeStruct((M, D), F32),
        scratch_shapes=[pltpu.VMEM((tm, D), BF16)],
        compiler_params=_cparams(("parallel", "arbitrary"), 58),
        name=f"ffn{sub}",
    )(x, mods, gpre.reshape(1, D), gpost.reshape(1, D), wg, wu, wd)


def _proj_kernel(x_ref, m_ref, gpre_ref, w_ref, o_ref, xn_ref):
    @pl.when(pl.program_id(1) == 0)
    def _():
        xn_ref[...] = _prenorm(x_ref[...], m_ref[...], gpre_ref[...], 1).astype(BF16)

    o_ref[...] = _dot(xn_ref[...], w_ref[...]).astype(o_ref.dtype)


def _proj(x, mods, l, gpre, w, S, tn):
    M, D = x.shape
    N = w.shape[-1]
    tm = min(512, S)
    tpb = S // tm
    return pl.pallas_call(
        _proj_kernel,
        grid=(M // tm, N // tn),
        in_specs=[
            pl.BlockSpec((tm, D), lambda i, j: (i, 0), pipeline_mode=pl.Buffered(1)),
            pl.BlockSpec((None, None, N_MOD, D), lambda i, j: (l, i // tpb, 0, 0)),
            pl.BlockSpec((1, D), lambda i, j: (0, 0)),
            pl.BlockSpec((None, D, tn), lambda i, j: (l, 0, j)),
        ],
        out_specs=pl.BlockSpec((tm, tn), lambda i, j: (i, j)),
        out_shape=jax.ShapeDtypeStruct((M, N), BF16),
        scratch_shapes=[pltpu.VMEM((tm, D), BF16)],
        compiler_params=_cparams(("parallel", "arbitrary"), 48),
        name="mixer_proj",
    )(x, mods, gpre.reshape(1, D), w)


class _Layout:
    def __init__(self, D, lru_w, attn_w, kv_w, idx_w, pool_w, tn):
        split = (lru_w, lru_w, attn_w, kv_w, kv_w, idx_w, IDX_DIM, IDX_HEADS, pool_w, 3 * D)
        names = ("u_lru", "u_gate", "q", "k", "v", "q_idx", "k_idx", "w_idx", "u_pool", "g_br")
        src = dict(zip(names, zip(np.cumsum((0,) + split[:-1]).tolist(), split)))
        segs = [("g_br", D, ["g_br"]), ("u_lru", lru_w, ["u_lru"]), ("u_gate", lru_w, ["u_gate"]),
                ("q", attn_w, ["q"]), ("u_pool", pool_w, ["u_pool"]), ("q_idx", idx_w, ["q_idx"]),
                ("k", kv_w, ["k"]), ("v", kv_w, ["v"]), ("kw", LANES, ["k_idx", "w_idx"])]
        segs.sort(key=lambda s: -s[1])
        self.off = {}
        self.pieces = []
        pos = 0
        for name, width, parts in segs:
            assert pos % width == 0
            self.off[name] = pos
            length = 0
            for p in parts:
                self.pieces.append(src[p])
                length += src[p][1]
            padded = -(-length // width) * width
            if padded != length:
                self.pieces.append((None, padded - length))
            pos += padded
        self.n = -(-pos // tn) * tn
        if self.n != pos:
            self.pieces.append((None, self.n - pos))
        self.d_in = sum(split)

    def pack(self, w_in):
        rows = w_in.shape[:-1]
        cols = [jnp.zeros(rows + (n,), BF16) if s is None else w_in[..., s:s + n].astype(BF16)
                for s, n in self.pieces]
        return jnp.concatenate(cols, axis=-1)


def _rope3(t, tab_ref, shift):
    return (t * tab_ref[0] + pltpu.roll(t, LANES - shift, 1) * tab_ref[1]
            + pltpu.roll(t, shift, 1) * tab_ref[2])


def _prep_kernel(q_ref, k_ref, qi_ref, kw_ref, tab_ref, qo_ref, ko_ref, qio_ref, kio_ref, wo_ref):
    scale = float(HEAD_DIM) ** -0.5
    for h in range(q_ref.shape[1] // LANES):
        sl = slice(h * LANES, (h + 1) * LANES)
        qo_ref[:, sl] = (_rope3(q_ref[:, sl].astype(F32), tab_ref.at[0], ROT_DIM // 2) * scale).astype(BF16)
    for h in range(k_ref.shape[1] // LANES):
        sl = slice(h * LANES, (h + 1) * LANES)
        ko_ref[:, sl] = _rope3(k_ref[:, sl].astype(F32), tab_ref.at[0], ROT_DIM // 2).astype(BF16)
    for h in range(qi_ref.shape[1] // LANES):
        sl = slice(h * LANES, (h + 1) * LANES)
        qio_ref[:, sl] = _rope3(qi_ref[:, sl].astype(F32), tab_ref.at[1], IDX_ROT_DIM // 2).astype(BF16)
    r = _rope3(kw_ref[...].astype(F32), tab_ref.at[2], IDX_ROT_DIM // 2)
    swapped = pltpu.roll(r, IDX_DIM, 1)
    lane = lax.broadcasted_iota(jnp.int32, r.shape, 1)
    kio_ref[...] = jnp.where(lane < IDX_DIM, r, swapped).astype(BF16)
    wo_ref[...] = swapped


def _prep(proj, tabs, lay, attn_w, kv_w, idx_w, S):
    M = proj.shape[0]
    tm = min(512, S)

    def col(name, width):
        c = lay.off[name] // width
        return pl.BlockSpec((tm, width), lambda i: (i, c))

    row = lambda width: pl.BlockSpec((tm, width), lambda i: (i, 0))
    return pl.pallas_call(
        _prep_kernel,
        grid=(M // tm,),
        in_specs=[col("q", attn_w), col("k", kv_w), col("q_idx", idx_w), col("kw", LANES),
                  pl.BlockSpec((3, 3, tm, LANES), lambda i: (0, 0, i, 0))],
        out_specs=[row(attn_w), row(kv_w), row(idx_w), row(LANES), row(LANES)],
        out_shape=[jax.ShapeDtypeStruct((M, attn_w), BF16), jax.ShapeDtypeStruct((M, kv_w), BF16),
                   jax.ShapeDtypeStruct((M, idx_w), BF16), jax.ShapeDtypeStruct((M, LANES), BF16),
                   jax.ShapeDtypeStruct((M, LANES), F32)],
        compiler_params=_cparams(("parallel",), 32),
        name="rope_prep",
    )(proj, proj, proj, proj, tabs)


LRU_HALO = 8
POOL_HALO = 16


def _gelu_tanh(x):
    return 0.5 * x * (1.0 + jnp.tanh(0.7978845608028654 * (x + 0.044715 * (x * x * x))))


def _seq_kernel(ul_ref, ug_ref, up_ref, cw_ref, cb_ref, wa_ref, ba_ref, wx_ref, bx_ref, lam_ref,
                pw_ref, ps_ref, lo_ref, po_ref, ext_l, ext_p, h_ref, a_s, b_s, *, ts):
    s = pl.program_id(1)

    @pl.when(s == 0)
    def _():
        ext_l[0:LRU_HALO, :] = jnp.zeros((LRU_HALO, ext_l.shape[1]), F32)
        ext_p[0:POOL_HALO, :] = jnp.zeros((POOL_HALO, ext_p.shape[1]), F32)
        h_ref[...] = jnp.zeros_like(h_ref)

    ext_l[LRU_HALO:LRU_HALO + ts, :] = ul_ref[...].astype(F32)
    nblk, bw = wa_ref.shape[0], wa_ref.shape[1]
    for h in range(nblk):
        sl = slice(h * bw, (h + 1) * bw)
        xc = cb_ref[:, sl]
        for j in range(CONV_WIDTH):
            xc = xc + cw_ref[j:j + 1, sl] * ext_l[pl.ds(LRU_HALO - (CONV_WIDTH - 1) + j, ts), sl]
        xb = xc.astype(BF16)
        r = jax.nn.sigmoid(_dot(xb, wa_ref[h]) + ba_ref[:, sl])
        i = jax.nn.sigmoid(_dot(xb, wx_ref[h]) + bx_ref[:, sl])
        nl = -lam_ref[:, sl]
        softplus = jnp.maximum(nl, 0.0) + jnp.log1p(jnp.exp(-jnp.abs(nl)))
        a = jnp.exp((-LRU_C) * r * softplus)
        a_s[:, sl] = a
        b_s[:, sl] = jnp.sqrt(1.0 - a * a) * (i * xc)
    ext_l[0:LRU_HALO, :] = ext_l[ts:ts + LRU_HALO, :]

    def step(t, h):
        h = a_s[pl.ds(t, 1), :] * h + b_s[pl.ds(t, 1), :]
        b_s[pl.ds(t, 1), :] = h
        return h

    h_ref[...] = lax.fori_loop(0, ts, step, h_ref[...], unroll=8)
    lo_ref[...] = (b_s[...] * _gelu_tanh(ug_ref[...].astype(F32))).astype(BF16)

    ext_p[POOL_HALO:POOL_HALO + ts, :] = up_ref[...].astype(F32)
    pg = pw_ref.shape[1]
    t_glob = s * ts + lax.broadcasted_iota(jnp.int32, (ts, 1), 0)
    for g, w in enumerate(POOL_WINDOWS):
        sl = slice(g * pg, (g + 1) * pg)
        tot = ext_p[POOL_HALO:POOL_HALO + ts, sl]
        cur = tot
        for j in range(1, w):
            tot = tot + ext_p[pl.ds(POOL_HALO - j, ts), sl]
        cnt = jnp.minimum(t_glob + 1, w).astype(F32)
        pooled = (tot / cnt - cur).astype(BF16)
        po_ref[:, sl] = (_dot(pooled, pw_ref[g]) * ps_ref[:, sl]).astype(BF16)
    ext_p[0:POOL_HALO, :] = ext_p[ts:ts + POOL_HALO, :]


def _seq(proj, lay, B, S, conv_w, conv_b, w_a, b_a, w_x, b_x, lam, pool_w, pool_scale):
    M = proj.shape[0]
    lw = conv_w.shape[-1]
    pw = pool_scale.shape[-1]
    ts = min(256, S)
    nt = S // ts

    def col(name, width):
        c = lay.off[name] // width
        return pl.BlockSpec((ts, width), lambda b, s: (b * nt + s, c))

    full = lambda a: pl.BlockSpec(a.shape, lambda b, s: (0,) * a.ndim)
    small = [conv_w, conv_b.reshape(1, lw), w_a, b_a.reshape(1, lw), w_x, b_x.reshape(1, lw),
             lam.reshape(1, lw), pool_w, pool_scale.reshape(1, pw)]
    return pl.pallas_call(
        functools.partial(_seq_kernel, ts=ts),
        grid=(B, nt),
        in_specs=[col("u_lru", lw), col("u_gate", lw), col("u_pool", pw)] + [full(a) for a in small],
        out_specs=[pl.BlockSpec((ts, lw), lambda b, s: (b * nt + s, 0)),
                   pl.BlockSpec((ts, pw), lambda b, s: (b * nt + s, 0))],
        out_shape=[jax.ShapeDtypeStruct((M, lw), BF16), jax.ShapeDtypeStruct((M, pw), BF16)],
        scratch_shapes=[pltpu.VMEM((LRU_HALO + ts, lw), F32), pltpu.VMEM((POOL_HALO + ts, pw), F32),
                        pltpu.VMEM((1, lw), F32), pltpu.VMEM((ts, lw), F32), pltpu.VMEM((ts, lw), F32)],
        compiler_params=_cparams(("arbitrary", "arbitrary"), 48),
        name="lru_pool",
    )(proj, proj, proj, *small)


def _attn_kernel(q_ref, qi_ref, w_ref, k_ref, v_ref, ki_ref, o_ref, keys_ref, *, tq, tk, topk):
    qb = pl.program_id(1)
    n_chunks = ((qb + 1) * tq + tk - 1) // tk
    row_pos = qb * tq + lax.broadcasted_iota(jnp.int32, (tq, 1), 0)
    low_half = lax.broadcasted_iota(jnp.int32, (1, LANES), 1) < IDX_DIM
    high_half = lax.broadcasted_iota(jnp.int32, (1, LANES), 1) >= IDX_DIM
    n_pairs = qi_ref.shape[1] // LANES

    def score_chunk(c, carry):
        start = pl.multiple_of(c * tk, tk)
        kic = ki_ref[pl.ds(start, tk), :]
        acc = jnp.zeros((tq, tk), F32)
        for p in range(n_pairs):
            qp = qi_ref[:, p * LANES:(p + 1) * LANES]
            for half in range(2):
                qm = jnp.where(low_half if half == 0 else high_half, qp, jnp.zeros_like(qp))
                hd = 2 * p + half
                acc = acc + w_ref[:, hd:hd + 1] * jnp.maximum(_dot_nt(qm, kic), 0.0)
        bits = lax.bitcast_convert_type(acc, jnp.int32)
        key = bits ^ ((bits >> 31) & 0x7FFFFFFF)
        col_pos = start + lax.broadcasted_iota(jnp.int32, (1, tk), 1)
        keys_ref[c] = jnp.where(col_pos <= row_pos, key, INT_MIN)
        return carry

    lax.fori_loop(0, n_chunks, score_chunk, 0)

    def bisect(it, thr):
        cand = thr + jnp.left_shift(jnp.int32(1), 31 - it)

        def count_chunk(c, acc):
            hit = jnp.where(keys_ref[c] >= cand, 1.0, 0.0)
            for u in range(tk // LANES):
                acc = acc + hit[:, u * LANES:(u + 1) * LANES]
            return acc

        part = lax.fori_loop(0, n_chunks, count_chunk, jnp.zeros((tq, LANES), F32))
        cnt = jnp.sum(part, axis=-1, keepdims=True)
        return jnp.where(cnt >= float(topk), cand, thr)

    thr = lax.fori_loop(0, 32, bisect, jnp.full((tq, 1), INT_MIN, jnp.int32))
    thr = jnp.maximum(thr, INT_MIN + 1)

    group = q_ref.shape[1] // (N_KV_HEADS * HEAD_DIM)
    for g in range(N_KV_HEADS):
        qg = jnp.concatenate(
            [q_ref[:, (g * group + r) * HEAD_DIM:(g * group + r + 1) * HEAD_DIM] for r in range(group)],
            axis=0)
        kv = slice(g * HEAD_DIM, (g + 1) * HEAD_DIM)

        def attend(c, carry):
            m_i, l_i, acc = carry
            start = pl.multiple_of(c * tk, tk)
            logits = _dot_nt(qg, k_ref[pl.ds(start, tk), kv])
            bias = jnp.where(keys_ref[c] >= thr, 0.0, MASK_NEG)
            logits = (logits.reshape(group, tq, tk) + bias[None]).reshape(group * tq, tk)
            m_new = jnp.maximum(m_i, jnp.max(logits, axis=-1, keepdims=True))
            alpha = jnp.exp(m_i - m_new)
            p = jnp.exp(logits - m_new)
            l_new = alpha * l_i + jnp.sum(p, axis=-1, keepdims=True)
            acc = alpha * acc + _dot(p.astype(BF16), v_ref[pl.ds(start, tk), kv])
            return m_new, l_new, acc

        init = (jnp.full((group * tq, 1), MASK_NEG, F32), jnp.zeros((group * tq, 1), F32),
                jnp.zeros((group * tq, HEAD_DIM), F32))
        _, l_f, acc = lax.fori_loop(0, n_chunks, attend, init)
        out = acc / l_f
        for r in range(group):
            hd = g * group + r
            o_ref[:, hd * HEAD_DIM:(hd + 1) * HEAD_DIM] = out[r * tq:(r + 1) * tq].astype(BF16)


def _attention(q, qi, w, k, proj, ki, lay, B, S):
    M, attn_w = q.shape
    kv_w = k.shape[1]
    idx_w = qi.shape[1]
    tq = min(256, S)
    tk = min(512, S)
    nq = S // tq
    topk = min(TOPK_MAX, S // 4)
    v_col = lay.off["v"] // kv_w
    return pl.pallas_call(
        functools.partial(_attn_kernel, tq=tq, tk=tk, topk=topk),
        grid=(B, nq),
        in_specs=[
            pl.BlockSpec((tq, attn_w), lambda b, i: (b * nq + i, 0)),
            pl.BlockSpec((tq, idx_w), lambda b, i: (b * nq + i, 0)),
            pl.BlockSpec((tq, LANES), lambda b, i: (b * nq + i, 0)),
            pl.BlockSpec((S, kv_w), lambda b, i: (b, 0)),
            pl.BlockSpec((S, kv_w), lambda b, i: (b, v_col)),
            pl.BlockSpec((S, LANES), lambda b, i: (b, 0)),
        ],
        out_specs=pl.BlockSpec((tq, attn_w), lambda b, i: (b * nq + i, 0)),
        out_shape=jax.ShapeDtypeStruct((M, attn_w), BF16),
        scratch_shapes=[pltpu.VMEM((S // tk, tq, tk), jnp.int32)],
        compiler_params=_cparams(("parallel", "arbitrary"), 56),
        name="indexer_attention",
    )(q, qi, w, k, proj, ki)


def _merge_kernel(x_ref, m_ref, gpost_ref, al_ref, aa_ref, ap_ref, g0_ref, g1_ref, g2_ref,
                  wl_ref, wa_ref, wp_ref, wo_ref, o_ref):
    j = pl.program_id(1)

    @pl.when(j == 0)
    def _():
        o_ref[...] = jnp.zeros_like(o_ref)

    merged = (jax.nn.sigmoid(g0_ref[...].astype(F32)) * _dot(al_ref[...], wl_ref[...])
              + jax.nn.sigmoid(g1_ref[...].astype(F32)) * _dot(aa_ref[...], wa_ref[...])
              + jax.nn.sigmoid(g2_ref[...].astype(F32)) * _dot(ap_ref[...], wp_ref[...]))
    o_ref[...] += _dot(merged.astype(BF16), wo_ref[...])

    @pl.when(j == pl.num_programs(1) - 1)
    def _():
        gate = m_ref[5:6, :]
        o_ref[...] = x_ref[...] + gate * (_rms(o_ref[...]) * gpost_ref[...])


def _merge(x, mods, l, gpost, a_lru, a_attn, a_pool, proj, lay, w_l, w_a, w_p, w_o, S):
    M, D = x.shape
    tm = min(512, S)
    tn = min(256, D)
    tpb = S // tm
    g_base = lay.off["g_br"] // tn
    nj = D // tn

    def gspec(br):
        return pl.BlockSpec((tm, tn), lambda i, j: (i, g_base + br * nj + j))

    def aspec(a):
        return pl.BlockSpec((tm, a.shape[1]), lambda i, j: (i, 0))

    def wspec(w):
        return pl.BlockSpec((None, w.shape[1], tn), lambda i, j: (l, 0, j))

    return pl.pallas_call(
        _merge_kernel,
        grid=(M // tm, nj),
        in_specs=[
            pl.BlockSpec((tm, D), lambda i, j: (i, 0), pipeline_mode=pl.Buffered(1)),
            pl.BlockSpec((None, None, N_MOD, D), lambda i, j: (l, i // tpb, 0, 0)),
            pl.BlockSpec((1, D), lambda i, j: (0, 0)),
            aspec(a_lru), aspec(a_attn), aspec(a_pool), gspec(0), gspec(1), gspec(2),
            wspec(w_l), wspec(w_a), wspec(w_p),
            pl.BlockSpec((None, tn, D), lambda i, j: (l, j, 0)),
        ],
        out_specs=pl.BlockSpec((tm, D), lambda i, j: (i, 0)),
        out_shape=jax.ShapeDtypeStruct((M, D), F32),
        compiler_params=_cparams(("parallel", "arbitrary"), 58),
        name="merge_out",
    )(x, mods, gpost.reshape(1, D), a_lru, a_attn, a_pool, proj, proj, proj, w_l, w_a, w_p, w_o)


def kernel(x, c, positions, w_mod, b_mod, mod_offset, norm_pre, norm_post, ffn1_w_gate, ffn1_w_up, ffn1_w_down, w_in, conv_w, conv_b, lru_w_a, lru_b_a, lru_w_x, lru_b_x, lru_lambda, pool_w, pool_scale, w_br_lru, w_br_attn, w_br_pool, w_out, ffn2_w_gate, ffn2_w_up, ffn2_w_down):
    B, S, D = x.shape
    M = B * S
    depth = w_in.shape[0]
    lru_w = conv_w.shape[-1]
    attn_w = w_br_attn.shape[1]
    kv_w = N_KV_HEADS * HEAD_DIM
    idx_w = IDX_HEADS * IDX_DIM
    pool_wd = pool_scale.shape[-1]
    tn_proj = 768
    lay = _Layout(D, lru_w, attn_w, kv_w, idx_w, pool_wd, tn_proj)
    assert lay.d_in == w_in.shape[-1]
    assert B <= MOD_ROWS and S % LANES == 0

    bf = lambda w: w.astype(BF16)
    w_in_p = lay.pack(w_in)
    f1g, f1u, f1d = bf(ffn1_w_gate), bf(ffn1_w_up), bf(ffn1_w_down)
    f2g, f2u, f2d = bf(ffn2_w_gate), bf(ffn2_w_up), bf(ffn2_w_down)
    wbl, wba, wbp, wo = bf(w_br_lru), bf(w_br_attn), bf(w_br_pool), bf(w_out)
    wa, wx, pw = bf(lru_w_a), bf(lru_w_x), bf(pool_w)

    mods = _modulation(c, w_mod, b_mod, mod_offset)
    tabs = _rope_tables(positions)

    xf = x.reshape(M, D)
    for l in range(depth):
        xf = _ffn(xf, mods, l, 0, norm_pre[l, 0], norm_post[l, 0], f1g, f1u, f1d, S)
        proj = _proj(xf, mods, l, norm_pre[l, 1], w_in_p, S, tn_proj)
        q, k, qi, ki, w = _prep(proj, tabs, lay, attn_w, kv_w, idx_w, S)
        a_lru, a_pool = _seq(proj, lay, B, S, conv_w[l], conv_b[l], wa[l], lru_b_a[l], wx[l], lru_b_x[l],
                             lru_lambda[l], pw[l], pool_scale[l])
        a_attn = _attention(q, qi, w, k, proj, ki, lay, B, S)
        xf = _merge(xf, mods, l, norm_post[l, 1], a_lru, a_attn, a_pool, proj, lay, wbl, wba, wbp, wo, S)
        xf = _ffn(xf, mods, l, 2, norm_pre[l, 2], norm_post[l, 2], f2g, f2u, f2d, S)
    return xf.reshape(B, S, D)
```

```python
import functools

import numpy as np
import jax
import jax.numpy as jnp
from jax import lax
from jax.experimental import pallas as pl
from jax.experimental.pallas import tpu as pltpu

F32 = jnp.float32
BF16 = jnp.bfloat16

N_MOD = 9
FFN_RES = 0.5
RMS_EPS = 1e-6
CONV_WIDTH = 4
LRU_C = 8.0
HEAD_DIM = 128
N_KV_HEADS = 4
ROT_DIM = HEAD_DIM // 4
IDX_HEADS = 16
IDX_DIM = 64
IDX_ROT_DIM = IDX_DIM // 4
TOPK_MAX = 256
ROPE_THETA = 500000.0
POOL_WINDOWS = (2, 4, 8, 16)

LANES = 128
MOD_ROWS = 8
MIB = 1024 * 1024
INT_MIN = -2 ** 31
MASK_NEG = -1e30


def _cparams(semantics, vmem_mib):
    return pltpu.CompilerParams(dimension_semantics=semantics, vmem_limit_bytes=vmem_mib * MIB)


def _dot(a, b):
    return jnp.dot(a, b, preferred_element_type=F32)


def _dot_nt(a, b):
    return lax.dot_general(a, b, (((1,), (1,)), ((), ())), preferred_element_type=F32)


def _mod_kernel(c_ref, w_ref, b_ref, off_ref, o_ref):
    c = c_ref[...]
    s = (c * jax.nn.sigmoid(c)).astype(BF16)
    base = _dot(s, w_ref[...].astype(BF16)) + b_ref[...]
    for l in range(o_ref.shape[0]):
        o_ref[l] = base + off_ref[l:l + 1, :]


def _modulation(c, w_mod, b_mod, mod_offset):
    B, D = c.shape
    depth = mod_offset.shape[0]
    n = w_mod.shape[1]
    tn = min(1024, n)
    c_pad = jnp.zeros((MOD_ROWS, D), F32).at[:B].set(c)
    out = pl.pallas_call(
        _mod_kernel,
        grid=(n // tn,),
        in_specs=[
            pl.BlockSpec((MOD_ROWS, D), lambda j: (0, 0)),
            pl.BlockSpec((D, tn), lambda j: (0, j)),
            pl.BlockSpec((1, tn), lambda j: (0, j)),
            pl.BlockSpec((depth, tn), lambda j: (0, j)),
        ],
        out_specs=pl.BlockSpec((depth, MOD_ROWS, tn), lambda j: (0, 0, j)),
        out_shape=jax.ShapeDtypeStruct((depth, MOD_ROWS, n), F32),
        compiler_params=_cparams(("arbitrary",), 48),
        name="modulation",
    )(c_pad, w_mod, b_mod.reshape(1, n), mod_offset.reshape(depth, n))
    return out.reshape(depth, MOD_ROWS, N_MOD, D)


def _tables_kernel(pos_ref, pat_ref, o_ref):
    pos = pos_ref[...]
    for t in range(3):
        ang = pos * pat_ref[t, 0:1, :]
        cos = jnp.cos(ang)
        sin = jnp.sin(ang)
        o_ref[t, 0] = cos * pat_ref[t, 1:2, :] + pat_ref[t, 2:3, :]
        o_ref[t, 1] = sin * pat_ref[t, 3:4, :]
        o_ref[t, 2] = sin * pat_ref[t, 4:5, :]


def _rope_pattern(head_dim, rot_dim, n_lanes):
    half = rot_dim // 2
    inv_freq = ROPE_THETA ** (-jnp.arange(half, dtype=F32) / half)
    d = np.arange(n_lanes) % head_dim
    in_rot = d < rot_dim
    invf = jnp.where(in_rot, inv_freq[d % half], 0.0)
    rows = [invf,
            jnp.asarray(in_rot, F32),
            jnp.asarray(~in_rot, F32),
            jnp.asarray(-(d < half).astype(np.float32)),
            jnp.asarray(((d >= half) & in_rot).astype(np.float32))]
    return jnp.stack(rows + [jnp.zeros((n_lanes,), F32)] * 3)


def _rope_tables(positions):
    M = positions.size
    attn = _rope_pattern(HEAD_DIM, ROT_DIM, LANES)
    idx = _rope_pattern(IDX_DIM, IDX_ROT_DIM, LANES)
    lane = np.arange(LANES)
    is_k = jnp.asarray(lane < IDX_DIM, F32)
    is_w = (lane >= IDX_DIM) & (lane < IDX_DIM + IDX_HEADS)
    w_scale = float(IDX_HEADS * IDX_DIM) ** -0.5
    kw = idx * is_k[None, :]
    kw = kw.at[2].set(idx[2] * is_k + jnp.asarray(is_w, F32) * w_scale)
    pat = jnp.stack([attn, idx, kw])
    tm = min(512, M)
    return pl.pallas_call(
        _tables_kernel,
        grid=(M // tm,),
        in_specs=[pl.BlockSpec((tm, 1), lambda i: (i, 0)),
                  pl.BlockSpec((3, 8, LANES), lambda i: (0, 0, 0))],
        out_specs=pl.BlockSpec((3, 3, tm, LANES), lambda i: (0, 0, i, 0)),
        out_shape=jax.ShapeDtypeStruct((3, 3, M, LANES), F32),
        compiler_params=_cparams(("parallel",), 32),
        name="rope_tables",
    )(positions.reshape(M, 1).astype(F32), pat)


NORM_ROWS = 64


def _prenorm_to(x_ref, dst_ref, m_ref, gpre_ref, sub):
    mul = gpre_ref[...] * (1.0 + m_ref[3 * sub + 1:3 * sub + 2, :])
    shift = m_ref[3 * sub:3 * sub + 1, :]

    def body(r, carry):
        rows = pl.ds(pl.multiple_of(r * NORM_ROWS, NORM_ROWS), NORM_ROWS)
        x = x_ref[rows, :]
        rs = lax.rsqrt(jnp.mean(x * x, axis=-1, keepdims=True) + RMS_EPS)
        dst_ref[rows, :] = (x_ref[rows, :] * rs * mul + shift).astype(BF16)
        return carry

    lax.fori_loop(0, x_ref.shape[0] // NORM_ROWS, body, 0)


def _postnorm_residual(x_ref, o_ref, m_ref, gpost_ref, sub, res):
    mul = (res * m_ref[3 * sub + 2:3 * sub + 3, :]) * gpost_ref[...]

    def body(r, carry):
        rows = pl.ds(pl.multiple_of(r * NORM_ROWS, NORM_ROWS), NORM_ROWS)
        y = o_ref[rows, :]
        rs = lax.rsqrt(jnp.mean(y * y, axis=-1, keepdims=True) + RMS_EPS)
        o_ref[rows, :] = x_ref[rows, :] + o_ref[rows, :] * rs * mul
        return carry

    lax.fori_loop(0, x_ref.shape[0] // NORM_ROWS, body, 0)


def _ffn_kernel(x_ref, m_ref, gpre_ref, gpost_ref, wg_ref, wu_ref, wd_ref, o_ref, xn_ref, *, sub):
    j = pl.program_id(1)

    @pl.when(j == 0)
    def _():
        _prenorm_to(x_ref, xn_ref, m_ref, gpre_ref, sub)
        o_ref[...] = jnp.zeros_like(o_ref)

    xn = xn_ref[...]
    h = _dot(xn, wg_ref[...])
    u = _dot(xn, wu_ref[...])
    o_ref[...] += _dot((h * jax.nn.sigmoid(h) * u).astype(BF16), wd_ref[...])

    @pl.when(j == pl.num_programs(1) - 1)
    def _():
        _postnorm_residual(x_ref, o_ref, m_ref, gpost_ref, sub, FFN_RES)


def _ffn(x, mods, l, sub, gpre, gpost, wg, wu, wd, S):
    M, D = x.shape
    F = wg.shape[-1]
    tm = min(512, S)
    tf = min(512, F)
    tpb = S // tm
    return pl.pallas_call(
        functools.partial(_ffn_kernel, sub=sub),
        grid=(M // tm, F // tf),
        in_specs=[
            pl.BlockSpec((tm, D), lambda i, j: (i, 0), pipeline_mode=pl.Buffered(1)),
            pl.BlockSpec((None, None, N_MOD, D), lambda i, j: (l, i // tpb, 0, 0)),
            pl.BlockSpec((1, D), lambda i, j: (0, 0)),
            pl.BlockSpec((1, D), lambda i, j: (0, 0)),
            pl.BlockSpec((None, D, tf), lambda i, j: (l, 0, j)),
            pl.BlockSpec((None, D, tf), lambda i, j: (l, 0, j)),
            pl.BlockSpec((None, tf, D), lambda i, j: (l, j, 0)),
        ],
        out_specs=pl.BlockSpec((tm, D), lambda i, j: (i, 0)),
        out_shape=jax.ShapeDtypeStruct((M, D), F32),
        scratch_shapes=[pltpu.VMEM((tm, D), BF16)],
        compiler_params=_cparams(("parallel", "arbitrary"), 60),
        name=f"ffn{sub}",
    )(x, mods, gpre.reshape(1, D), gpost.reshape(1, D), wg, wu, wd)


def _proj_kernel(x_ref, m_ref, gpre_ref, w_ref, o_ref, xn_ref):
    @pl.when(pl.program_id(1) == 0)
    def _():
        _prenorm_to(x_ref, xn_ref, m_ref, gpre_ref, 1)

    o_ref[...] = _dot(xn_ref[...], w_ref[...]).astype(o_ref.dtype)


def _proj(x, mods, l, gpre, w, S, tn):
    M, D = x.shape
    N = w.shape[-1]
    tm = min(1024, S)
    tpb = S // tm
    return pl.pallas_call(
        _proj_kernel,
        grid=(M // tm, N // tn),
        in_specs=[
            pl.BlockSpec((tm, D), lambda i, j: (i, 0), pipeline_mode=pl.Buffered(1)),
            pl.BlockSpec((None, None, N_MOD, D), lambda i, j: (l, i // tpb, 0, 0)),
            pl.BlockSpec((1, D), lambda i, j: (0, 0)),
            pl.BlockSpec((None, D, tn), lambda i, j: (l, 0, j)),
        ],
        out_specs=pl.BlockSpec((tm, tn), lambda i, j: (i, j)),
        out_shape=jax.ShapeDtypeStruct((M, N), BF16),
        scratch_shapes=[pltpu.VMEM((tm, D), BF16)],
        compiler_params=_cparams(("parallel", "arbitrary"), 60),
        name="mixer_proj",
    )(x, mods, gpre.reshape(1, D), w)


class _Layout:
    def __init__(self, D, lru_w, attn_w, kv_w, idx_w, pool_w, tn):
        split = (lru_w, lru_w, attn_w, kv_w, kv_w, idx_w, IDX_DIM, IDX_HEADS, pool_w, 3 * D)
        names = ("u_lru", "u_gate", "q", "k", "v", "q_idx", "k_idx", "w_idx", "u_pool", "g_br")
        src = dict(zip(names, zip(np.cumsum((0,) + split[:-1]).tolist(), split)))
        segs = [("u_lru", lru_w, ["u_lru"]), ("u_gate", lru_w, ["u_gate"]), ("q", attn_w, ["q"]),
                ("u_pool", pool_w, ["u_pool"]), ("g_br", D, ["g_br"]), ("k", kv_w, ["k"]),
                ("v", kv_w, ["v"]), ("q_idx", idx_w, ["q_idx"]), ("kw", LANES, ["k_idx", "w_idx"])]

        def place(order):
            off, pieces, pos = {}, [], 0
            for name, width, parts in order:
                if pos % width:
                    return None
                off[name] = pos
                length = 0
                for p in parts:
                    pieces.append(src[p])
                    length += src[p][1]
                padded = -(-length // width) * width
                if padded != length:
                    pieces.append((None, padded - length))
                pos += padded
            return off, pieces, pos

        self.off, pieces, pos = place(segs) or place(sorted(segs, key=lambda s: -s[1]))
        self.n = -(-pos // tn) * tn
        if self.n != pos:
            pieces.append((None, self.n - pos))
        self.pieces = []
        for start, length in pieces:
            last = self.pieces[-1] if self.pieces else None
            if last and (start is None) == (last[0] is None) and (start is None or last[0] + last[1] == start):
                self.pieces[-1] = (last[0], last[1] + length)
            else:
                self.pieces.append((start, length))
        self.d_in = sum(split)

    def pack(self, w_in):
        rows = w_in.shape[:-1]
        cols = [jnp.zeros(rows + (n,), BF16) if s is None else w_in[..., s:s + n].astype(BF16)
                for s, n in self.pieces]
        return jnp.concatenate(cols, axis=-1)


def _rope3(t, tab_ref, shift):
    return (t * tab_ref[0] + pltpu.roll(t, LANES - shift, 1) * tab_ref[1]
            + pltpu.roll(t, shift, 1) * tab_ref[2])


def _prep_kernel(q_ref, k_ref, qi_ref, kw_ref, tab_ref, qo_ref, ko_ref, qio_ref, kio_ref, wo_ref):
    scale = float(HEAD_DIM) ** -0.5
    for h in range(q_ref.shape[1] // LANES):
        sl = slice(h * LANES, (h + 1) * LANES)
        qo_ref[:, sl] = (_rope3(q_ref[:, sl].astype(F32), tab_ref.at[0], ROT_DIM // 2) * scale).astype(BF16)
    for h in range(k_ref.shape[1] // LANES):
        sl = slice(h * LANES, (h + 1) * LANES)
        ko_ref[:, sl] = _rope3(k_ref[:, sl].astype(F32), tab_ref.at[0], ROT_DIM // 2).astype(BF16)
    for h in range(qi_ref.shape[1] // LANES):
        sl = slice(h * LANES, (h + 1) * LANES)
        qio_ref[:, sl] = _rope3(qi_ref[:, sl].astype(F32), tab_ref.at[1], IDX_ROT_DIM // 2).astype(BF16)
    r = _rope3(kw_ref[...].astype(F32), tab_ref.at[2], IDX_ROT_DIM // 2)
    swapped = pltpu.roll(r, IDX_DIM, 1)
    lane = lax.broadcasted_iota(jnp.int32, r.shape, 1)
    kio_ref[...] = jnp.where(lane < IDX_DIM, r, swapped).astype(BF16)
    wo_ref[...] = swapped


def _prep(proj, tabs, lay, attn_w, kv_w, idx_w, S):
    M = proj.shape[0]
    tm = min(512, S)

    def col(name, width):
        c = lay.off[name] // width
        return pl.BlockSpec((tm, width), lambda i: (i, c))

    row = lambda width: pl.BlockSpec((tm, width), lambda i: (i, 0))
    return pl.pallas_call(
        _prep_kernel,
        grid=(M // tm,),
        in_specs=[col("q", attn_w), col("k", kv_w), col("q_idx", idx_w), col("kw", LANES),
                  pl.BlockSpec((3, 3, tm, LANES), lambda i: (0, 0, i, 0))],
        out_specs=[row(attn_w), row(kv_w), row(idx_w), row(LANES), row(LANES)],
        out_shape=[jax.ShapeDtypeStruct((M, attn_w), BF16), jax.ShapeDtypeStruct((M, kv_w), BF16),
                   jax.ShapeDtypeStruct((M, idx_w), BF16), jax.ShapeDtypeStruct((M, LANES), BF16),
                   jax.ShapeDtypeStruct((M, LANES), F32)],
        compiler_params=_cparams(("parallel",), 32),
        name="rope_prep",
    )(proj, proj, proj, proj, tabs)


LRU_HALO = 8
POOL_HALO = 16


def _gelu_tanh(x):
    return 0.5 * x * (1.0 + jnp.tanh(0.7978845608028654 * (x + 0.044715 * (x * x * x))))


def _seq_kernel(ul_ref, ug_ref, up_ref, cw_ref, cb_ref, wa_ref, ba_ref, wx_ref, bx_ref, lam_ref,
                pw_ref, ps_ref, lo_ref, po_ref, ext_l, ext_p, h_ref, a_s, b_s, *, ts):
    s = pl.program_id(1)

    @pl.when(s == 0)
    def _():
        ext_l[0:LRU_HALO, :] = jnp.zeros((LRU_HALO, ext_l.shape[1]), F32)
        ext_p[0:POOL_HALO, :] = jnp.zeros((POOL_HALO, ext_p.shape[1]), F32)
        h_ref[...] = jnp.zeros_like(h_ref)

    ext_l[LRU_HALO:LRU_HALO + ts, :] = ul_ref[...].astype(F32)
    nblk, bw = wa_ref.shape[0], wa_ref.shape[1]
    for h in range(nblk):
        sl = slice(h * bw, (h + 1) * bw)
        xc = cb_ref[:, sl]
        for j in range(CONV_WIDTH):
            xc = xc + cw_ref[j:j + 1, sl] * ext_l[pl.ds(LRU_HALO - (CONV_WIDTH - 1) + j, ts), sl]
        xb = xc.astype(BF16)
        r = jax.nn.sigmoid(_dot(xb, wa_ref[h]) + ba_ref[:, sl])
        i = jax.nn.sigmoid(_dot(xb, wx_ref[h]) + bx_ref[:, sl])
        nl = -lam_ref[:, sl]
        softplus = jnp.maximum(nl, 0.0) + jnp.log1p(jnp.exp(-jnp.abs(nl)))
        a = jnp.exp((-LRU_C) * r * softplus)
        a_s[:, sl] = a
        b_s[:, sl] = jnp.sqrt(1.0 - a * a) * (i * xc)
    ext_l[0:LRU_HALO, :] = ext_l[ts:ts + LRU_HALO, :]

    def step(t, h):
        h = a_s[pl.ds(t, 1), :] * h + b_s[pl.ds(t, 1), :]
        b_s[pl.ds(t, 1), :] = h
        return h

    h_ref[...] = lax.fori_loop(0, ts, step, h_ref[...], unroll=8)
    lo_ref[...] = (b_s[...] * _gelu_tanh(ug_ref[...].astype(F32))).astype(BF16)

    ext_p[POOL_HALO:POOL_HALO + ts, :] = up_ref[...].astype(F32)
    pg = pw_ref.shape[1]
    t_glob = s * ts + lax.broadcasted_iota(jnp.int32, (ts, 1), 0)
    for g, w in enumerate(POOL_WINDOWS):
        sl = slice(g * pg, (g + 1) * pg)
        tot = ext_p[POOL_HALO:POOL_HALO + ts, sl]
        cur = tot
        for j in range(1, w):
            tot = tot + ext_p[pl.ds(POOL_HALO - j, ts), sl]
        cnt = jnp.minimum(t_glob + 1, w).astype(F32)
        pooled = (tot / cnt - cur).astype(BF16)
        po_ref[:, sl] = (_dot(pooled, pw_ref[g]) * ps_ref[:, sl]).astype(BF16)
    ext_p[0:POOL_HALO, :] = ext_p[ts:ts + POOL_HALO, :]


def _seq(proj, lay, B, S, conv_w, conv_b, w_a, b_a, w_x, b_x, lam, pool_w, pool_scale):
    M = proj.shape[0]
    lw = conv_w.shape[-1]
    pw = pool_scale.shape[-1]
    ts = min(256, S)
    nt = S // ts

    def col(name, width):
        c = lay.off[name] // width
        return pl.BlockSpec((ts, width), lambda b, s: (b * nt + s, c))

    full = lambda a: pl.BlockSpec(a.shape, lambda b, s: (0,) * a.ndim)
    small = [conv_w, conv_b.reshape(1, lw), w_a, b_a.reshape(1, lw), w_x, b_x.reshape(1, lw),
             lam.reshape(1, lw), pool_w, pool_scale.reshape(1, pw)]
    return pl.pallas_call(
        functools.partial(_seq_kernel, ts=ts),
        grid=(B, nt),
        in_specs=[col("u_lru", lw), col("u_gate", lw), col("u_pool", pw)] + [full(a) for a in small],
        out_specs=[pl.BlockSpec((ts, lw), lambda b, s: (b * nt + s, 0)),
                   pl.BlockSpec((ts, pw), lambda b, s: (b * nt + s, 0))],
        out_shape=[jax.ShapeDtypeStruct((M, lw), BF16), jax.ShapeDtypeStruct((M, pw), BF16)],
        scratch_shapes=[pltpu.VMEM((LRU_HALO + ts, lw), F32), pltpu.VMEM((POOL_HALO + ts, pw), F32),
                        pltpu.VMEM((1, lw), F32), pltpu.VMEM((ts, lw), F32), pltpu.VMEM((ts, lw), F32)],
        compiler_params=_cparams(("arbitrary", "arbitrary"), 48),
        name="lru_pool",
    )(proj, proj, proj, *small)


def _attn_kernel(q_ref, qi_ref, w_ref, k_ref, v_ref, ki_ref, o_ref, keys_ref, *, tq, tk, topk):
    qb = pl.program_id(1)
    n_chunks = ((qb + 1) * tq + tk - 1) // tk
    row_pos = qb * tq + lax.broadcasted_iota(jnp.int32, (tq, 1), 0)
    low_half = lax.broadcasted_iota(jnp.int32, (1, LANES), 1) < IDX_DIM
    high_half = lax.broadcasted_iota(jnp.int32, (1, LANES), 1) >= IDX_DIM
    n_pairs = qi_ref.shape[1] // LANES

    def score_chunk(c, carry):
        start = pl.multiple_of(c * tk, tk)
        kic = ki_ref[pl.ds(start, tk), :]
        acc = jnp.zeros((tq, tk), F32)
        for p in range(n_pairs):
            qp = qi_ref[:, p * LANES:(p + 1) * LANES]
            for half in range(2):
                qm = jnp.where(low_half if half == 0 else high_half, qp, jnp.zeros_like(qp))
                hd = 2 * p + half
                acc = acc + w_ref[:, hd:hd + 1] * jnp.maximum(_dot_nt(qm, kic), 0.0)
        bits = lax.bitcast_convert_type(acc, jnp.int32)
        key = bits ^ ((bits >> 31) & 0x7FFFFFFF)
        col_pos = start + lax.broadcasted_iota(jnp.int32, (1, tk), 1)
        keys_ref[c] = jnp.where(col_pos <= row_pos, key, INT_MIN)
        return carry

    lax.fori_loop(0, n_chunks, score_chunk, 0)

    def bisect(it, thr):
        cand = thr + jnp.left_shift(jnp.int32(1), 31 - it)

        def count_chunk(c, acc):
            hit = jnp.where(keys_ref[c] >= cand, 1.0, 0.0)
            for u in range(tk // LANES):
                acc = acc + hit[:, u * LANES:(u + 1) * LANES]
            return acc

        part = lax.fori_loop(0, n_chunks, count_chunk, jnp.zeros((tq, LANES), F32))
        cnt = jnp.sum(part, axis=-1, keepdims=True)
        return jnp.where(cnt >= float(topk), cand, thr)

    thr = lax.fori_loop(0, 32, bisect, jnp.full((tq, 1), INT_MIN, jnp.int32))
    thr = jnp.maximum(thr, INT_MIN + 1)

    group = q_ref.shape[1] // (N_KV_HEADS * HEAD_DIM)
    for g in range(N_KV_HEADS):
        qg = jnp.concatenate(
            [q_ref[:, (g * group + r) * HEAD_DIM:(g * group + r + 1) * HEAD_DIM] for r in range(group)],
            axis=0)
        kv = slice(g * HEAD_DIM, (g + 1) * HEAD_DIM)

        def attend(c, carry):
            m_i, l_i, acc = carry
            start = pl.multiple_of(c * tk, tk)
            logits = _dot_nt(qg, k_ref[pl.ds(start, tk), kv])
            bias = jnp.where(keys_ref[c] >= thr, 0.0, MASK_NEG)
            logits = (logits.reshape(group, tq, tk) + bias[None]).reshape(group * tq, tk)
            m_new = jnp.maximum(m_i, jnp.max(logits, axis=-1, keepdims=True))
            alpha = jnp.exp(m_i - m_new)
            p = jnp.exp(logits - m_new)
            l_new = alpha * l_i + jnp.sum(p, axis=-1, keepdims=True)
            acc = alpha * acc + _dot(p.astype(BF16), v_ref[pl.ds(start, tk), kv])
            return m_new, l_new, acc

        init = (jnp.full((group * tq, 1), MASK_NEG, F32), jnp.zeros((group * tq, 1), F32),
                jnp.zeros((group * tq, HEAD_DIM), F32))
        _, l_f, acc = lax.fori_loop(0, n_chunks, attend, init)
        out = acc / l_f
        for r in range(group):
            hd = g * group + r
            o_ref[:, hd * HEAD_DIM:(hd + 1) * HEAD_DIM] = out[r * tq:(r + 1) * tq].astype(BF16)


def _attention(q, qi, w, k, proj, ki, lay, B, S):
    M, attn_w = q.shape
    kv_w = k.shape[1]
    idx_w = qi.shape[1]
    tq = min(256, S)
    tk = min(512, S)
    nq = S // tq
    topk = min(TOPK_MAX, S // 4)
    v_col = lay.off["v"] // kv_w
    return pl.pallas_call(
        functools.partial(_attn_kernel, tq=tq, tk=tk, topk=topk),
        grid=(B, nq),
        in_specs=[
            pl.BlockSpec((tq, attn_w), lambda b, i: (b * nq + i, 0)),
            pl.BlockSpec((tq, idx_w), lambda b, i: (b * nq + i, 0)),
            pl.BlockSpec((tq, LANES), lambda b, i: (b * nq + i, 0)),
            pl.BlockSpec((S, kv_w), lambda b, i: (b, 0)),
            pl.BlockSpec((S, kv_w), lambda b, i: (b, v_col)),
            pl.BlockSpec((S, LANES), lambda b, i: (b, 0)),
        ],
        out_specs=pl.BlockSpec((tq, attn_w), lambda b, i: (b * nq + i, 0)),
        out_shape=jax.ShapeDtypeStruct((M, attn_w), BF16),
        scratch_shapes=[pltpu.VMEM((S // tk, tq, tk), jnp.int32)],
        compiler_params=_cparams(("parallel", "arbitrary"), 56),
        name="indexer_attention",
    )(q, qi, w, k, proj, ki)


def _branch_kernel(al_ref, aa_ref, ap_ref, g0_ref, g1_ref, g2_ref, wl_ref, wa_ref, wp_ref, o_ref):
    merged = (jax.nn.sigmoid(g0_ref[...].astype(F32)) * _dot(al_ref[...], wl_ref[...])
              + jax.nn.sigmoid(g1_ref[...].astype(F32)) * _dot(aa_ref[...], wa_ref[...])
              + jax.nn.sigmoid(g2_ref[...].astype(F32)) * _dot(ap_ref[...], wp_ref[...]))
    o_ref[...] = merged.astype(o_ref.dtype)


def _branch_merge(l, a_lru, a_attn, a_pool, proj, lay, w_l, w_a, w_p, S):
    M = a_lru.shape[0]
    D = w_l.shape[-1]
    tm = min(1024, S)
    tn = min(512, D)
    g_base = lay.off["g_br"] // tn
    nj = D // tn

    def gspec(br):
        return pl.BlockSpec((tm, tn), lambda i, j: (i, g_base + br * nj + j))

    def aspec(a):
        return pl.BlockSpec((tm, a.shape[1]), lambda i, j: (i, 0), pipeline_mode=pl.Buffered(1))

    def wspec(w):
        return pl.BlockSpec((None, w.shape[1], tn), lambda i, j: (l, 0, j))

    return pl.pallas_call(
        _branch_kernel,
        grid=(M // tm, nj),
        in_specs=[aspec(a_lru), aspec(a_attn), aspec(a_pool), gspec(0), gspec(1), gspec(2),
                  wspec(w_l), wspec(w_a), wspec(w_p)],
        out_specs=pl.BlockSpec((tm, tn), lambda i, j: (i, j)),
        out_shape=jax.ShapeDtypeStruct((M, D), BF16),
        compiler_params=_cparams(("parallel", "arbitrary"), 56),
        name="branch_merge",
    )(a_lru, a_attn, a_pool, proj, proj, proj, w_l, w_a, w_p)


def _outproj_kernel(x_ref, m_ref, gpost_ref, a_ref, w_ref, o_ref):
    j = pl.program_id(1)

    @pl.when(j == 0)
    def _():
        o_ref[...] = jnp.zeros_like(o_ref)

    o_ref[...] += _dot(a_ref[...], w_ref[...])

    @pl.when(j == pl.num_programs(1) - 1)
    def _():
        _postnorm_residual(x_ref, o_ref, m_ref, gpost_ref, 1, 1.0)


def _outproj(x, mods, l, gpost, merged, w_o, S):
    M, D = x.shape
    K = merged.shape[1]
    tm = min(512, S)
    tk = min(1024, K)
    tpb = S // tm
    return pl.pallas_call(
        _outproj_kernel,
        grid=(M // tm, K // tk),
        in_specs=[
            pl.BlockSpec((tm, D), lambda i, j: (i, 0), pipeline_mode=pl.Buffered(1)),
            pl.BlockSpec((None, None, N_MOD, D), lambda i, j: (l, i // tpb, 0, 0)),
            pl.BlockSpec((1, D), lambda i, j: (0, 0)),
            pl.BlockSpec((tm, tk), lambda i, j: (i, j)),
            pl.BlockSpec((None, tk, D), lambda i, j: (l, j, 0)),
        ],
        out_specs=pl.BlockSpec((tm, D), lambda i, j: (i, 0)),
        out_shape=jax.ShapeDtypeStruct((M, D), F32),
        compiler_params=_cparams(("parallel", "arbitrary"), 56),
        name="out_proj",
    )(x, mods, gpost.reshape(1, D), merged, w_o)


def kernel(x, c, positions, w_mod, b_mod, mod_offset, norm_pre, norm_post, ffn1_w_gate, ffn1_w_up, ffn1_w_down, w_in, conv_w, conv_b, lru_w_a, lru_b_a, lru_w_x, lru_b_x, lru_lambda, pool_w, pool_scale, w_br_lru, w_br_attn, w_br_pool, w_out, ffn2_w_gate, ffn2_w_up, ffn2_w_down):
    B, S, D = x.shape
    M = B * S
    depth = w_in.shape[0]
    lru_w = conv_w.shape[-1]
    attn_w = w_br_attn.shape[1]
    kv_w = N_KV_HEADS * HEAD_DIM
    idx_w = IDX_HEADS * IDX_DIM
    pool_wd = pool_scale.shape[-1]
    tn_proj = 768
    lay = _Layout(D, lru_w, attn_w, kv_w, idx_w, pool_wd, tn_proj)
    assert lay.d_in == w_in.shape[-1]
    assert B <= MOD_ROWS and S % LANES == 0

    bf = lambda w: w.astype(BF16)
    w_in_p = lay.pack(w_in)
    f1g, f1u, f1d = bf(ffn1_w_gate), bf(ffn1_w_up), bf(ffn1_w_down)
    f2g, f2u, f2d = bf(ffn2_w_gate), bf(ffn2_w_up), bf(ffn2_w_down)
    wbl, wba, wbp, wo = bf(w_br_lru), bf(w_br_attn), bf(w_br_pool), bf(w_out)
    wa, wx, pw = bf(lru_w_a), bf(lru_w_x), bf(pool_w)

    mods = _modulation(c, w_mod, b_mod, mod_offset)
    tabs = _rope_tables(positions)

    xf = x.reshape(M, D)
    for l in range(depth):
        xf = _ffn(xf, mods, l, 0, norm_pre[l, 0], norm_post[l, 0], f1g, f1u, f1d, S)
        proj = _proj(xf, mods, l, norm_pre[l, 1], w_in_p, S, tn_proj)
        q, k, qi, ki, w = _prep(proj, tabs, lay, attn_w, kv_w, idx_w, S)
        a_lru, a_pool = _seq(proj, lay, B, S, conv_w[l], conv_b[l], wa[l], lru_b_a[l], wx[l], lru_b_x[l],
                             lru_lambda[l], pw[l], pool_scale[l])
        a_attn = _attention(q, qi, w, k, proj, ki, lay, B, S)
        merged = _branch_merge(l, a_lru, a_attn, a_pool, proj, lay, wbl, wba, wbp, S)
        xf = _outproj(xf, mods, l, norm_post[l, 1], merged, wo, S)
        xf = _ffn(xf, mods, l, 2, norm_pre[l, 2], norm_post[l, 2], f2g, f2u, f2d, S)
    return xf.reshape(B, S, D)
```

```python
import functools

import numpy as np
import jax
import jax.numpy as jnp
from jax import lax
from jax.experimental import pallas as pl
from jax.experimental.pallas import tpu as pltpu

F32 = jnp.float32
BF16 = jnp.bfloat16

N_MOD = 9
FFN_RES = 0.5
RMS_EPS = 1e-6
CONV_WIDTH = 4
LRU_C = 8.0
HEAD_DIM = 128
N_KV_HEADS = 4
ROT_DIM = HEAD_DIM // 4
IDX_HEADS = 16
IDX_DIM = 64
IDX_ROT_DIM = IDX_DIM // 4
TOPK_MAX = 256
ROPE_THETA = 500000.0
POOL_WINDOWS = (2, 4, 8, 16)

LANES = 128
MOD_ROWS = 8
MIB = 1024 * 1024
INT_MIN = -2 ** 31
MASK_NEG = -1e30


def _cparams(semantics, vmem_mib):
    return pltpu.CompilerParams(dimension_semantics=semantics, vmem_limit_bytes=vmem_mib * MIB)


def _dot(a, b):
    return jnp.dot(a, b, preferred_element_type=F32)


def _dot_nt(a, b):
    return lax.dot_general(a, b, (((1,), (1,)), ((), ())), preferred_element_type=F32)


def _mod_kernel(c_ref, w_ref, b_ref, off_ref, o_ref):
    c = c_ref[...]
    s = (c * jax.nn.sigmoid(c)).astype(BF16)
    base = _dot(s, w_ref[...].astype(BF16)) + b_ref[...]
    for l in range(o_ref.shape[0]):
        o_ref[l] = base + off_ref[l:l + 1, :]


def _modulation(c, w_mod, b_mod, mod_offset):
    B, D = c.shape
    depth = mod_offset.shape[0]
    n = w_mod.shape[1]
    tn = min(1024, n)
    c_pad = jnp.zeros((MOD_ROWS, D), F32).at[:B].set(c)
    out = pl.pallas_call(
        _mod_kernel,
        grid=(n // tn,),
        in_specs=[
            pl.BlockSpec((MOD_ROWS, D), lambda j: (0, 0)),
            pl.BlockSpec((D, tn), lambda j: (0, j)),
            pl.BlockSpec((1, tn), lambda j: (0, j)),
            pl.BlockSpec((depth, tn), lambda j: (0, j)),
        ],
        out_specs=pl.BlockSpec((depth, MOD_ROWS, tn), lambda j: (0, 0, j)),
        out_shape=jax.ShapeDtypeStruct((depth, MOD_ROWS, n), F32),
        compiler_params=_cparams(("arbitrary",), 48),
        name="modulation",
    )(c_pad, w_mod, b_mod.reshape(1, n), mod_offset.reshape(depth, n))
    return out.reshape(depth, MOD_ROWS, N_MOD, D)


def _tables_kernel(pos_ref, pat_ref, o_ref):
    pos = pos_ref[...]
    for t in range(3):
        ang = pos * pat_ref[t, 0:1, :]
        cos = jnp.cos(ang)
        sin = jnp.sin(ang)
        o_ref[t, 0] = cos * pat_ref[t, 1:2, :] + pat_ref[t, 2:3, :]
        o_ref[t, 1] = sin * pat_ref[t, 3:4, :]
        o_ref[t, 2] = sin * pat_ref[t, 4:5, :]


def _rope_pattern(head_dim, rot_dim, n_lanes):
    half = rot_dim // 2
    inv_freq = ROPE_THETA ** (-jnp.arange(half, dtype=F32) / half)
    d = np.arange(n_lanes) % head_dim
    in_rot = d < rot_dim
    invf = jnp.where(in_rot, inv_freq[d % half], 0.0)
    rows = [invf,
            jnp.asarray(in_rot, F32),
            jnp.asarray(~in_rot, F32),
            jnp.asarray(-(d < half).astype(np.float32)),
            jnp.asarray(((d >= half) & in_rot).astype(np.float32))]
    return jnp.stack(rows + [jnp.zeros((n_lanes,), F32)] * 3)


def _rope_tables(positions):
    M = positions.size
    attn = _rope_pattern(HEAD_DIM, ROT_DIM, LANES)
    idx = _rope_pattern(IDX_DIM, IDX_ROT_DIM, LANES)
    lane = np.arange(LANES)
    is_k = jnp.asarray(lane < IDX_DIM, F32)
    is_w = (lane >= IDX_DIM) & (lane < IDX_DIM + IDX_HEADS)
    w_scale = float(IDX_HEADS * IDX_DIM) ** -0.5
    kw = idx * is_k[None, :]
    kw = kw.at[2].set(idx[2] * is_k + jnp.asarray(is_w, F32) * w_scale)
    pat = jnp.stack([attn, idx, kw])
    tm = min(512, M)
    return pl.pallas_call(
        _tables_kernel,
        grid=(M // tm,),
        in_specs=[pl.BlockSpec((tm, 1), lambda i: (i, 0)),
                  pl.BlockSpec((3, 8, LANES), lambda i: (0, 0, 0))],
        out_specs=pl.BlockSpec((3, 3, tm, LANES), lambda i: (0, 0, i, 0)),
        out_shape=jax.ShapeDtypeStruct((3, 3, M, LANES), F32),
        compiler_params=_cparams(("parallel",), 32),
        name="rope_tables",
    )(positions.reshape(M, 1).astype(F32), pat)


NORM_ROWS = 64


def _prenorm_to(x_ref, dst_ref, m_ref, gpre_ref, sub):
    mul = gpre_ref[...] * (1.0 + m_ref[3 * sub + 1:3 * sub + 2, :])
    shift = m_ref[3 * sub:3 * sub + 1, :]

    def body(r, carry):
        rows = pl.ds(pl.multiple_of(r * NORM_ROWS, NORM_ROWS), NORM_ROWS)
        x = x_ref[rows, :]
        rs = lax.rsqrt(jnp.mean(x * x, axis=-1, keepdims=True) + RMS_EPS)
        dst_ref[rows, :] = (x_ref[rows, :] * rs * mul + shift).astype(BF16)
        return carry

    lax.fori_loop(0, x_ref.shape[0] // NORM_ROWS, body, 0)


def _postnorm_residual(x_ref, o_ref, m_ref, gpost_ref, sub, res):
    mul = (res * m_ref[3 * sub + 2:3 * sub + 3, :]) * gpost_ref[...]

    def body(r, carry):
        rows = pl.ds(pl.multiple_of(r * NORM_ROWS, NORM_ROWS), NORM_ROWS)
        y = o_ref[rows, :]
        rs = lax.rsqrt(jnp.mean(y * y, axis=-1, keepdims=True) + RMS_EPS)
        o_ref[rows, :] = x_ref[rows, :] + o_ref[rows, :] * rs * mul
        return carry

    lax.fori_loop(0, x_ref.shape[0] // NORM_ROWS, body, 0)


def _ffn_kernel(x_ref, m_ref, gpre_ref, gpost_ref, wg_ref, wu_ref, wd_ref, o_ref, xn_ref, *, sub):
    j = pl.program_id(1)

    @pl.when(j == 0)
    def _():
        _prenorm_to(x_ref, xn_ref, m_ref, gpre_ref, sub)
        o_ref[...] = jnp.zeros_like(o_ref)

    xn = xn_ref[...]
    h = _dot(xn, wg_ref[...])
    u = _dot(xn, wu_ref[...])
    o_ref[...] += _dot((h * jax.nn.sigmoid(h) * u).astype(BF16), wd_ref[...])

    @pl.when(j == pl.num_programs(1) - 1)
    def _():
        _postnorm_residual(x_ref, o_ref, m_ref, gpost_ref, sub, FFN_RES)


def _ffn(x, mods, l, sub, gpre, gpost, wg, wu, wd, S):
    M, D = x.shape
    F = wg.shape[-1]
    tm = min(512, S)
    tf = min(512, F)
    tpb = S // tm
    return pl.pallas_call(
        functools.partial(_ffn_kernel, sub=sub),
        grid=(M // tm, F // tf),
        in_specs=[
            pl.BlockSpec((tm, D), lambda i, j: (i, 0), pipeline_mode=pl.Buffered(1)),
            pl.BlockSpec((None, None, N_MOD, D), lambda i, j: (l, i // tpb, 0, 0)),
            pl.BlockSpec((1, D), lambda i, j: (0, 0)),
            pl.BlockSpec((1, D), lambda i, j: (0, 0)),
            pl.BlockSpec((None, D, tf), lambda i, j: (l, 0, j)),
            pl.BlockSpec((None, D, tf), lambda i, j: (l, 0, j)),
            pl.BlockSpec((None, tf, D), lambda i, j: (l, j, 0)),
        ],
        out_specs=pl.BlockSpec((tm, D), lambda i, j: (i, 0)),
        out_shape=jax.ShapeDtypeStruct((M, D), F32),
        scratch_shapes=[pltpu.VMEM((tm, D), BF16)],
        compiler_params=_cparams(("parallel", "arbitrary"), 60),
        name=f"ffn{sub}",
    )(x, mods, gpre.reshape(1, D), gpost.reshape(1, D), wg, wu, wd)


def _proj_kernel(x_ref, m_ref, gpre_ref, w_ref, o_ref, xn_ref):
    @pl.when(pl.program_id(1) == 0)
    def _():
        _prenorm_to(x_ref, xn_ref, m_ref, gpre_ref, 1)

    o_ref[...] = _dot(xn_ref[...], w_ref[...]).astype(o_ref.dtype)


def _proj(x, mods, l, gpre, w, S, tn):
    M, D = x.shape
    N = w.shape[-1]
    tm = min(1024, S)
    tpb = S // tm
    return pl.pallas_call(
        _proj_kernel,
        grid=(M // tm, N // tn),
        in_specs=[
            pl.BlockSpec((tm, D), lambda i, j: (i, 0), pipeline_mode=pl.Buffered(1)),
            pl.BlockSpec((None, None, N_MOD, D), lambda i, j: (l, i // tpb, 0, 0)),
            pl.BlockSpec((1, D), lambda i, j: (0, 0)),
            pl.BlockSpec((None, D, tn), lambda i, j: (l, 0, j)),
        ],
        out_specs=pl.BlockSpec((tm, tn), lambda i, j: (i, j)),
        out_shape=jax.ShapeDtypeStruct((M, N), BF16),
        scratch_shapes=[pltpu.VMEM((tm, D), BF16)],
        compiler_params=_cparams(("parallel", "arbitrary"), 60),
        name="mixer_proj",
    )(x, mods, gpre.reshape(1, D), w)


class _Layout:
    def __init__(self, D, lru_w, attn_w, kv_w, idx_w, pool_w, tn):
        split = (lru_w, lru_w, attn_w, kv_w, kv_w, idx_w, IDX_DIM, IDX_HEADS, pool_w, 3 * D)
        names = ("u_lru", "u_gate", "q", "k", "v", "q_idx", "k_idx", "w_idx", "u_pool", "g_br")
        src = dict(zip(names, zip(np.cumsum((0,) + split[:-1]).tolist(), split)))
        segs = [("u_lru", lru_w, ["u_lru"]), ("u_gate", lru_w, ["u_gate"]), ("q", attn_w, ["q"]),
                ("u_pool", pool_w, ["u_pool"]), ("g_br", D, ["g_br"]), ("k", kv_w, ["k"]),
                ("v", kv_w, ["v"]), ("q_idx", idx_w, ["q_idx"]), ("kw", LANES, ["k_idx", "w_idx"])]

        def place(order):
            off, pieces, pos = {}, [], 0
            for name, width, parts in order:
                if pos % width:
                    return None
                off[name] = pos
                length = 0
                for p in parts:
                    pieces.append(src[p])
                    length += src[p][1]
                padded = -(-length // width) * width
                if padded != length:
                    pieces.append((None, padded - length))
                pos += padded
            return off, pieces, pos

        self.off, pieces, pos = place(segs) or place(sorted(segs, key=lambda s: -s[1]))
        self.n = -(-pos // tn) * tn
        if self.n != pos:
            pieces.append((None, self.n - pos))
        self.pieces = []
        for start, length in pieces:
            last = self.pieces[-1] if self.pieces else None
            if last and (start is None) == (last[0] is None) and (start is None or last[0] + last[1] == start):
                self.pieces[-1] = (last[0], last[1] + length)
            else:
                self.pieces.append((start, length))
        self.d_in = sum(split)

    def pack_plan(self):
        spans, dest = [], 0
        for s, n in self.pieces:
            spans.append((dest, n, s))
            dest += n
        if self.n % PACK_COLS:
            return None
        plan, shift = [], 0
        for d0 in range(0, self.n, PACK_COLS):
            inside = [sp for sp in spans if sp[0] < d0 + PACK_COLS and sp[0] + sp[1] > d0]
            first, rest = inside[0], inside[1:]
            if any(sp[2] is not None for sp in rest):
                return None
            if first[2] is None:
                plan.append((0, 0, 0, 0))
                continue
            c = first[2] + d0 - first[0]
            sh = c % LANES
            if (c - sh) % PACK_COLS or (sh and shift not in (0, sh)):
                return None
            shift = shift or sh
            ncols = min(PACK_COLS, first[0] + first[1] - d0)
            plan.append(((c - sh) // PACK_COLS, (c - sh + PACK_COLS) // LANES if sh else 0, int(sh > 0), ncols))
        return plan, shift

    def pack(self, w_in):
        planned = self.pack_plan()
        if planned is not None and w_in.shape[1] % PACK_ROWS == 0:
            return _pack_columns(w_in, planned[0], planned[1], self.n)
        rows = w_in.shape[:-1]
        cols = [jnp.zeros(rows + (n,), BF16) if s is None else w_in[..., s:s + n].astype(BF16)
                for s, n in self.pieces]
        return jnp.concatenate(cols, axis=-1)


PACK_COLS = 512
PACK_ROWS = 2048


def _pack_kernel(a_idx, b_idx, shifted, ncols, a_ref, b_ref, o_ref, *, shift):
    j = pl.program_id(2)
    keep = lax.broadcasted_iota(jnp.int32, (1, PACK_COLS), 1) < ncols[j]

    @pl.when(shifted[j] == 0)
    def _():
        o_ref[...] = jnp.where(keep, a_ref[...], 0.0).astype(BF16)

    if shift:
        @pl.when(shifted[j] != 0)
        def _():
            wide = jnp.concatenate([a_ref[...], b_ref[...]], axis=1)
            o_ref[...] = jnp.where(keep, wide[:, shift:shift + PACK_COLS], 0.0).astype(BF16)


def _pack_columns(w, plan, shift, n_out):
    L, R, _ = w.shape
    cols = [jnp.asarray([p[k] for p in plan], jnp.int32) for k in range(4)]
    grid_spec = pltpu.PrefetchScalarGridSpec(
        num_scalar_prefetch=4,
        grid=(L, R // PACK_ROWS, n_out // PACK_COLS),
        in_specs=[pl.BlockSpec((None, PACK_ROWS, PACK_COLS), lambda l, i, j, a, b, s, n: (l, i, a[j])),
                  pl.BlockSpec((None, PACK_ROWS, LANES), lambda l, i, j, a, b, s, n: (l, i, b[j]))],
        out_specs=pl.BlockSpec((None, PACK_ROWS, PACK_COLS), lambda l, i, j, a, b, s, n: (l, i, j)),
    )
    return pl.pallas_call(
        functools.partial(_pack_kernel, shift=shift),
        grid_spec=grid_spec,
        out_shape=jax.ShapeDtypeStruct((L, R, n_out), BF16),
        compiler_params=_cparams(("parallel", "parallel", "arbitrary"), 40),
        name="pack_w_in",
    )(*cols, w, w)


def _cast_kernel(x_ref, o_ref):
    o_ref[...] = x_ref[...].astype(o_ref.dtype)


def _cast_bf16(w):
    L, R, C = w.shape
    tr, tc = min(1024, R), min(2048, C)
    assert R % tr == 0 and C % tc == 0
    spec = pl.BlockSpec((None, tr, tc), lambda l, i, j: (l, i, j))
    return pl.pallas_call(
        _cast_kernel,
        grid=(L, R // tr, C // tc),
        in_specs=[spec],
        out_specs=spec,
        out_shape=jax.ShapeDtypeStruct((L, R, C), BF16),
        compiler_params=_cparams(("parallel", "parallel", "parallel"), 40),
        name="cast_bf16",
    )(w)


def _rope3(t, tab_ref, shift):
    return (t * tab_ref[0] + pltpu.roll(t, LANES - shift, 1) * tab_ref[1]
            + pltpu.roll(t, shift, 1) * tab_ref[2])


def _prep_kernel(q_ref, k_ref, qi_ref, kw_ref, tab_ref, qo_ref, ko_ref, qio_ref, kio_ref, wo_ref):
    scale = float(HEAD_DIM) ** -0.5
    for h in range(q_ref.shape[1] // LANES):
        sl = slice(h * LANES, (h + 1) * LANES)
        qo_ref[:, sl] = (_rope3(q_ref[:, sl].astype(F32), tab_ref.at[0], ROT_DIM // 2) * scale).astype(BF16)
    for h in range(k_ref.shape[1] // LANES):
        sl = slice(h * LANES, (h + 1) * LANES)
        ko_ref[:, sl] = _rope3(k_ref[:, sl].astype(F32), tab_ref.at[0], ROT_DIM // 2).astype(BF16)
    for h in range(qi_ref.shape[1] // LANES):
        sl = slice(h * LANES, (h + 1) * LANES)
        qio_ref[:, sl] = _rope3(qi_ref[:, sl].astype(F32), tab_ref.at[1], IDX_ROT_DIM // 2).astype(BF16)
    r = _rope3(kw_ref[...].astype(F32), tab_ref.at[2], IDX_ROT_DIM // 2)
    swapped = pltpu.roll(r, IDX_DIM, 1)
    lane = lax.broadcasted_iota(jnp.int32, r.shape, 1)
    kio_ref[...] = jnp.where(lane < IDX_DIM, r, swapped).astype(BF16)
    wo_ref[...] = swapped


def _prep(proj, tabs, lay, attn_w, kv_w, idx_w, S):
    M = proj.shape[0]
    tm = min(512, S)

    def col(name, width):
        c = lay.off[name] // width
        return pl.BlockSpec((tm, width), lambda i: (i, c))

    row = lambda width: pl.BlockSpec((tm, width), lambda i: (i, 0))
    return pl.pallas_call(
        _prep_kernel,
        grid=(M // tm,),
        in_specs=[col("q", attn_w), col("k", kv_w), col("q_idx", idx_w), col("kw", LANES),
                  pl.BlockSpec((3, 3, tm, LANES), lambda i: (0, 0, i, 0))],
        out_specs=[row(attn_w), row(kv_w), row(idx_w), row(LANES), row(LANES)],
        out_shape=[jax.ShapeDtypeStruct((M, attn_w), BF16), jax.ShapeDtypeStruct((M, kv_w), BF16),
                   jax.ShapeDtypeStruct((M, idx_w), BF16), jax.ShapeDtypeStruct((M, LANES), BF16),
                   jax.ShapeDtypeStruct((M, LANES), F32)],
        compiler_params=_cparams(("parallel",), 32),
        name="rope_prep",
    )(proj, proj, proj, proj, tabs)


LRU_HALO = 8
POOL_HALO = 16


def _gelu_tanh(x):
    return 0.5 * x * (1.0 + jnp.tanh(0.7978845608028654 * (x + 0.044715 * (x * x * x))))


def _seq_kernel(ul_ref, ug_ref, up_ref, cw_ref, cb_ref, wa_ref, ba_ref, wx_ref, bx_ref, lam_ref,
                pw_ref, ps_ref, lo_ref, po_ref, ext_l, ext_p, h_ref, a_s, b_s, *, ts):
    s = pl.program_id(1)

    @pl.when(s == 0)
    def _():
        ext_l[0:LRU_HALO, :] = jnp.zeros((LRU_HALO, ext_l.shape[1]), F32)
        ext_p[0:POOL_HALO, :] = jnp.zeros((POOL_HALO, ext_p.shape[1]), F32)
        h_ref[...] = jnp.zeros_like(h_ref)

    ext_l[LRU_HALO:LRU_HALO + ts, :] = ul_ref[...].astype(F32)
    nblk, bw = wa_ref.shape[0], wa_ref.shape[1]
    for h in range(nblk):
        sl = slice(h * bw, (h + 1) * bw)
        xc = cb_ref[:, sl]
        for j in range(CONV_WIDTH):
            xc = xc + cw_ref[j:j + 1, sl] * ext_l[pl.ds(LRU_HALO - (CONV_WIDTH - 1) + j, ts), sl]
        xb = xc.astype(BF16)
        r = jax.nn.sigmoid(_dot(xb, wa_ref[h]) + ba_ref[:, sl])
        i = jax.nn.sigmoid(_dot(xb, wx_ref[h]) + bx_ref[:, sl])
        nl = -lam_ref[:, sl]
        softplus = jnp.maximum(nl, 0.0) + jnp.log1p(jnp.exp(-jnp.abs(nl)))
        a = jnp.exp((-LRU_C) * r * softplus)
        a_s[:, sl] = a
        b_s[:, sl] = jnp.sqrt(1.0 - a * a) * (i * xc)
    ext_l[0:LRU_HALO, :] = ext_l[ts:ts + LRU_HALO, :]

    def step(t, h):
        h = a_s[pl.ds(t, 1), :] * h + b_s[pl.ds(t, 1), :]
        b_s[pl.ds(t, 1), :] = h
        return h

    h_ref[...] = lax.fori_loop(0, ts, step, h_ref[...], unroll=8)
    lo_ref[...] = (b_s[...] * _gelu_tanh(ug_ref[...].astype(F32))).astype(BF16)

    ext_p[POOL_HALO:POOL_HALO + ts, :] = up_ref[...].astype(F32)
    pg = pw_ref.shape[1]
    t_glob = s * ts + lax.broadcasted_iota(jnp.int32, (ts, 1), 0)
    for g, w in enumerate(POOL_WINDOWS):
        sl = slice(g * pg, (g + 1) * pg)
        tot = ext_p[POOL_HALO:POOL_HALO + ts, sl]
        cur = tot
        for j in range(1, w):
            tot = tot + ext_p[pl.ds(POOL_HALO - j, ts), sl]
        cnt = jnp.minimum(t_glob + 1, w).astype(F32)
        pooled = (tot / cnt - cur).astype(BF16)
        po_ref[:, sl] = (_dot(pooled, pw_ref[g]) * ps_ref[:, sl]).astype(BF16)
    ext_p[0:POOL_HALO, :] = ext_p[ts:ts + POOL_HALO, :]


def _seq(proj, lay, B, S, conv_w, conv_b, w_a, b_a, w_x, b_x, lam, pool_w, pool_scale):
    M = proj.shape[0]
    lw = conv_w.shape[-1]
    pw = pool_scale.shape[-1]
    ts = min(256, S)
    nt = S // ts

    def col(name, width):
        c = lay.off[name] // width
        return pl.BlockSpec((ts, width), lambda b, s: (b * nt + s, c))

    full = lambda a: pl.BlockSpec(a.shape, lambda b, s: (0,) * a.ndim)
    small = [conv_w, conv_b.reshape(1, lw), w_a, b_a.reshape(1, lw), w_x, b_x.reshape(1, lw),
             lam.reshape(1, lw), pool_w, pool_scale.reshape(1, pw)]
    return pl.pallas_call(
        functools.partial(_seq_kernel, ts=ts),
        grid=(B, nt),
        in_specs=[col("u_lru", lw), col("u_gate", lw), col("u_pool", pw)] + [full(a) for a in small],
        out_specs=[pl.BlockSpec((ts, lw), lambda b, s: (b * nt + s, 0)),
                   pl.BlockSpec((ts, pw), lambda b, s: (b * nt + s, 0))],
        out_shape=[jax.ShapeDtypeStruct((M, lw), BF16), jax.ShapeDtypeStruct((M, pw), BF16)],
        scratch_shapes=[pltpu.VMEM((LRU_HALO + ts, lw), F32), pltpu.VMEM((POOL_HALO + ts, pw), F32),
                        pltpu.VMEM((1, lw), F32), pltpu.VMEM((ts, lw), F32), pltpu.VMEM((ts, lw), F32)],
        compiler_params=_cparams(("arbitrary", "arbitrary"), 48),
        name="lru_pool",
    )(proj, proj, proj, *small)


COUNT_ROWS = 64


def _attn_kernel(q_ref, qi_ref, w_ref, k_ref, v_ref, ki_ref, o_ref, keys_ref, qs_ref, m_ref, acc_ref,
                 *, tq, tk, topk):
    qb = pl.program_id(1)
    n_chunks = ((qb + 1) * tq + tk - 1) // tk
    row_pos = qb * tq + lax.broadcasted_iota(jnp.int32, (tq, 1), 0)
    low_half = lax.broadcasted_iota(jnp.int32, (1, LANES), 1) < IDX_DIM
    high_half = lax.broadcasted_iota(jnp.int32, (1, LANES), 1) >= IDX_DIM
    n_pairs = qi_ref.shape[1] // LANES

    def score_chunk(c, carry):
        start = pl.multiple_of(c * tk, tk)
        kic = ki_ref[pl.ds(start, tk), :]
        acc = jnp.zeros((tq, tk), F32)
        for p in range(n_pairs):
            qp = qi_ref[:, p * LANES:(p + 1) * LANES]
            for half in range(2):
                qm = jnp.where(low_half if half == 0 else high_half, qp, jnp.zeros_like(qp))
                hd = 2 * p + half
                acc = acc + w_ref[:, hd:hd + 1] * jnp.maximum(_dot_nt(qm, kic), 0.0)
        bits = lax.bitcast_convert_type(acc, jnp.int32)
        key = bits ^ ((bits >> 31) & 0x7FFFFFFF)
        col_pos = start + lax.broadcasted_iota(jnp.int32, (1, tk), 1)
        keys_ref[c] = jnp.where(col_pos <= row_pos, key, INT_MIN)
        return carry

    lax.fori_loop(0, n_chunks, score_chunk, 0)

    def bisect(it, thr):
        cand = thr + jnp.left_shift(jnp.int32(1), 31 - it)

        parts = []
        for rb in range(tq // COUNT_ROWS):
            rows = slice(rb * COUNT_ROWS, (rb + 1) * COUNT_ROWS)
            cand_b = jnp.broadcast_to(cand[rows], (COUNT_ROWS, LANES))

            def count_chunk(c, acc, rows=rows, cand_b=cand_b):
                for u in range(tk // LANES):
                    acc = acc + jnp.where(keys_ref[c, rows, u * LANES:(u + 1) * LANES] >= cand_b, 1.0, 0.0)
                return acc

            parts.append(lax.fori_loop(0, n_chunks, count_chunk, jnp.zeros((COUNT_ROWS, LANES), F32)))
        cnt = jnp.sum(jnp.concatenate(parts, axis=0), axis=-1, keepdims=True)
        return jnp.where(cnt >= float(topk), cand, thr)

    thr = lax.fori_loop(0, 32, bisect, jnp.full((tq, 1), INT_MIN, jnp.int32))
    thr = jnp.maximum(thr, INT_MIN + 1)

    def selection_bias(c):
        return jnp.where(keys_ref[c] >= thr, 0.0, MASK_NEG)

    n_heads = q_ref.shape[1] // HEAD_DIM
    group = n_heads // N_KV_HEADS
    gr = group * tq
    n_sub = tk // LANES
    for h in range(n_heads):
        qs_ref[h * tq:(h + 1) * tq, :] = q_ref[:, h * HEAD_DIM:(h + 1) * HEAD_DIM]

    def masked_logits(c, bias, g):
        start = pl.multiple_of(c * tk, tk)
        logits = _dot_nt(qs_ref[g * gr:(g + 1) * gr, :], k_ref[pl.ds(start, tk), g * HEAD_DIM:(g + 1) * HEAD_DIM])
        return (logits.reshape(group, tq, tk) + bias[None]).reshape(gr, tk)

    m_ref[...] = jnp.full(m_ref.shape, MASK_NEG, F32)

    def max_chunk(c, carry):
        bias = selection_bias(c)
        for g in range(N_KV_HEADS):
            logits = masked_logits(c, bias, g)
            m_part = m_ref[g * gr:(g + 1) * gr, :]
            for u in range(n_sub):
                m_part = jnp.maximum(m_part, logits[:, u * LANES:(u + 1) * LANES])
            m_ref[g * gr:(g + 1) * gr, :] = m_part
        return carry

    lax.fori_loop(0, n_chunks, max_chunk, 0)
    for g in range(N_KV_HEADS):
        rows = slice(g * gr, (g + 1) * gr)
        m_ref[rows, :] = jnp.broadcast_to(jnp.max(m_ref[rows, :], axis=-1, keepdims=True), (gr, LANES))

    acc_ref[...] = jnp.zeros(acc_ref.shape, F32)

    def acc_chunk(c, carry):
        start = pl.multiple_of(c * tk, tk)
        bias = selection_bias(c)
        ones = jnp.ones((tk, HEAD_DIM), BF16)
        for g in range(N_KV_HEADS):
            rows = slice(g * gr, (g + 1) * gr)
            logits = masked_logits(c, bias, g)
            m_row = m_ref[rows, :]
            p = jnp.concatenate([jnp.exp(logits[:, u * LANES:(u + 1) * LANES] - m_row) for u in range(n_sub)],
                                axis=1).astype(BF16)
            v_ext = jnp.concatenate([v_ref[pl.ds(start, tk), g * HEAD_DIM:(g + 1) * HEAD_DIM], ones], axis=1)
            acc_ref[rows, :] += _dot(p, v_ext)
        return carry

    lax.fori_loop(0, n_chunks, acc_chunk, 0)
    for h in range(n_heads):
        rows = slice(h * tq, (h + 1) * tq)
        o_ref[:, h * HEAD_DIM:(h + 1) * HEAD_DIM] = (
            acc_ref[rows, :HEAD_DIM] / acc_ref[rows, HEAD_DIM:HEAD_DIM + 1]).astype(BF16)


def _attention(q, qi, w, k, proj, ki, lay, B, S):
    M, attn_w = q.shape
    kv_w = k.shape[1]
    idx_w = qi.shape[1]
    tq = min(256, S)
    tk = min(512, S)
    nq = S // tq
    topk = min(TOPK_MAX, S // 4)
    v_col = lay.off["v"] // kv_w
    return pl.pallas_call(
        functools.partial(_attn_kernel, tq=tq, tk=tk, topk=topk),
        grid=(B, nq),
        in_specs=[
            pl.BlockSpec((tq, attn_w), lambda b, i: (b * nq + i, 0)),
            pl.BlockSpec((tq, idx_w), lambda b, i: (b * nq + i, 0)),
            pl.BlockSpec((tq, LANES), lambda b, i: (b * nq + i, 0)),
            pl.BlockSpec((S, kv_w), lambda b, i: (b, 0)),
            pl.BlockSpec((S, kv_w), lambda b, i: (b, v_col)),
            pl.BlockSpec((S, LANES), lambda b, i: (b, 0)),
        ],
        out_specs=pl.BlockSpec((tq, attn_w), lambda b, i: (b * nq + i, 0)),
        out_shape=jax.ShapeDtypeStruct((M, attn_w), BF16),
        scratch_shapes=[pltpu.VMEM((S // tk, tq, tk), jnp.int32),
                        pltpu.VMEM((attn_w // HEAD_DIM * tq, HEAD_DIM), BF16),
                        pltpu.VMEM((attn_w // HEAD_DIM * tq, LANES), F32),
                        pltpu.VMEM((attn_w // HEAD_DIM * tq, 2 * HEAD_DIM), F32)],
        compiler_params=_cparams(("parallel", "arbitrary"), 56),
        name="indexer_attention",
    )(q, qi, w, k, proj, ki)


def _branch_kernel(al_ref, aa_ref, ap_ref, g0_ref, g1_ref, g2_ref, wl_ref, wa_ref, wp_ref, o_ref):
    merged = (jax.nn.sigmoid(g0_ref[...].astype(F32)) * _dot(al_ref[...], wl_ref[...])
              + jax.nn.sigmoid(g1_ref[...].astype(F32)) * _dot(aa_ref[...], wa_ref[...])
              + jax.nn.sigmoid(g2_ref[...].astype(F32)) * _dot(ap_ref[...], wp_ref[...]))
    o_ref[...] = merged.astype(o_ref.dtype)


def _branch_merge(l, a_lru, a_attn, a_pool, proj, lay, w_l, w_a, w_p, S):
    M = a_lru.shape[0]
    D = w_l.shape[-1]
    tm = min(1024, S)
    tn = min(512, D)
    g_base = lay.off["g_br"] // tn
    nj = D // tn

    def gspec(br):
        return pl.BlockSpec((tm, tn), lambda i, j: (i, g_base + br * nj + j))

    def aspec(a):
        return pl.BlockSpec((tm, a.shape[1]), lambda i, j: (i, 0), pipeline_mode=pl.Buffered(1))

    def wspec(w):
        return pl.BlockSpec((None, w.shape[1], tn), lambda i, j: (l, 0, j))

    return pl.pallas_call(
        _branch_kernel,
        grid=(M // tm, nj),
        in_specs=[aspec(a_lru), aspec(a_attn), aspec(a_pool), gspec(0), gspec(1), gspec(2),
                  wspec(w_l), wspec(w_a), wspec(w_p)],
        out_specs=pl.BlockSpec((tm, tn), lambda i, j: (i, j)),
        out_shape=jax.ShapeDtypeStruct((M, D), BF16),
        compiler_params=_cparams(("parallel", "arbitrary"), 56),
        name="branch_merge",
    )(a_lru, a_attn, a_pool, proj, proj, proj, w_l, w_a, w_p)


def _outproj_kernel(x_ref, m_ref, gpost_ref, a_ref, w_ref, o_ref):
    j = pl.program_id(1)

    @pl.when(j == 0)
    def _():
        o_ref[...] = jnp.zeros_like(o_ref)

    o_ref[...] += _dot(a_ref[...], w_ref[...])

    @pl.when(j == pl.num_programs(1) - 1)
    def _():
        _postnorm_residual(x_ref, o_ref, m_ref, gpost_ref, 1, 1.0)


def _outproj(x, mods, l, gpost, merged, w_o, S):
    M, D = x.shape
    K = merged.shape[1]
    tm = min(512, S)
    tk = min(1024, K)
    tpb = S // tm
    return pl.pallas_call(
        _outproj_kernel,
        grid=(M // tm, K // tk),
        in_specs=[
            pl.BlockSpec((tm, D), lambda i, j: (i, 0), pipeline_mode=pl.Buffered(1)),
            pl.BlockSpec((None, None, N_MOD, D), lambda i, j: (l, i // tpb, 0, 0)),
            pl.BlockSpec((1, D), lambda i, j: (0, 0)),
            pl.BlockSpec((tm, tk), lambda i, j: (i, j)),
            pl.BlockSpec((None, tk, D), lambda i, j: (l, j, 0)),
        ],
        out_specs=pl.BlockSpec((tm, D), lambda i, j: (i, 0)),
        out_shape=jax.ShapeDtypeStruct((M, D), F32),
        compiler_params=_cparams(("parallel", "arbitrary"), 56),
        name="out_proj",
    )(x, mods, gpost.reshape(1, D), merged, w_o)


def kernel(x, c, positions, w_mod, b_mod, mod_offset, norm_pre, norm_post, ffn1_w_gate, ffn1_w_up, ffn1_w_down, w_in, conv_w, conv_b, lru_w_a, lru_b_a, lru_w_x, lru_b_x, lru_lambda, pool_w, pool_scale, w_br_lru, w_br_attn, w_br_pool, w_out, ffn2_w_gate, ffn2_w_up, ffn2_w_down):
    B, S, D = x.shape
    M = B * S
    depth = w_in.shape[0]
    lru_w = conv_w.shape[-1]
    attn_w = w_br_attn.shape[1]
    kv_w = N_KV_HEADS * HEAD_DIM
    idx_w = IDX_HEADS * IDX_DIM
    pool_wd = pool_scale.shape[-1]
    tn_proj = 768
    lay = _Layout(D, lru_w, attn_w, kv_w, idx_w, pool_wd, tn_proj)
    assert lay.d_in == w_in.shape[-1]
    assert B <= MOD_ROWS and S % LANES == 0

    bf = _cast_bf16
    w_in_p = lay.pack(w_in)
    f1g, f1u, f1d = bf(ffn1_w_gate), bf(ffn1_w_up), bf(ffn1_w_down)
    f2g, f2u, f2d = bf(ffn2_w_gate), bf(ffn2_w_up), bf(ffn2_w_down)
    wbl, wba, wbp, wo = bf(w_br_lru), bf(w_br_attn), bf(w_br_pool), bf(w_out)
    wa, wx, pw = lru_w_a.astype(BF16), lru_w_x.astype(BF16), pool_w.astype(BF16)

    mods = _modulation(c, w_mod, b_mod, mod_offset)
    tabs = _rope_tables(positions)

    xf = x.reshape(M, D)
    for l in range(depth):
        xf = _ffn(xf, mods, l, 0, norm_pre[l, 0], norm_post[l, 0], f1g, f1u, f1d, S)
        proj = _proj(xf, mods, l, norm_pre[l, 1], w_in_p, S, tn_proj)
        q, k, qi, ki, w = _prep(proj, tabs, lay, attn_w, kv_w, idx_w, S)
        a_lru, a_pool = _seq(proj, lay, B, S, conv_w[l], conv_b[l], wa[l], lru_b_a[l], wx[l], lru_b_x[l],
                             lru_lambda[l], pw[l], pool_scale[l])
        a_attn = _attention(q, qi, w, k, proj, ki, lay, B, S)
        merged = _branch_merge(l, a_lru, a_attn, a_pool, proj, lay, wbl, wba, wbp, S)
        xf = _outproj(xf, mods, l, norm_post[l, 1], merged, wo, S)
        xf = _ffn(xf, mods, l, 2, norm_pre[l, 2], norm_post[l, 2], f2g, f2u, f2d, S)
    return xf.reshape(B, S, D)
```

```python
import functools

import numpy as np
import jax
import jax.numpy as jnp
from jax import lax
from jax.experimental import pallas as pl
from jax.experimental.pallas import tpu as pltpu

F32 = jnp.float32
BF16 = jnp.bfloat16

N_MOD = 9
FFN_RES = 0.5
RMS_EPS = 1e-6
CONV_WIDTH = 4
LRU_C = 8.0
HEAD_DIM = 128
N_KV_HEADS = 4
ROT_DIM = HEAD_DIM // 4
IDX_HEADS = 16
IDX_DIM = 64
IDX_ROT_DIM = IDX_DIM // 4
TOPK_MAX = 256
ROPE_THETA = 500000.0
POOL_WINDOWS = (2, 4, 8, 16)

LANES = 128
MOD_ROWS = 8
MIB = 1024 * 1024
INT_MIN = -2 ** 31
MASK_NEG = -1e30


def _cparams(semantics, vmem_mib):
    return pltpu.CompilerParams(dimension_semantics=semantics, vmem_limit_bytes=vmem_mib * MIB)


def _dot(a, b):
    return jnp.dot(a, b, preferred_element_type=F32)


def _dot_nt(a, b):
    return lax.dot_general(a, b, (((1,), (1,)), ((), ())), preferred_element_type=F32)


def _mod_kernel(c_ref, w_ref, b_ref, off_ref, o_ref):
    c = c_ref[...]
    s = (c * jax.nn.sigmoid(c)).astype(BF16)
    base = _dot(s, w_ref[...].astype(BF16)) + b_ref[...]
    for l in range(o_ref.shape[0]):
        o_ref[l] = base + off_ref[l:l + 1, :]


def _modulation(c, w_mod, b_mod, mod_offset):
    B, D = c.shape
    depth = mod_offset.shape[0]
    n = w_mod.shape[1]
    tn = min(1024, n)
    c_pad = jnp.zeros((MOD_ROWS, D), F32).at[:B].set(c)
    out = pl.pallas_call(
        _mod_kernel,
        grid=(n // tn,),
        in_specs=[
            pl.BlockSpec((MOD_ROWS, D), lambda j: (0, 0)),
            pl.BlockSpec((D, tn), lambda j: (0, j)),
            pl.BlockSpec((1, tn), lambda j: (0, j)),
            pl.BlockSpec((depth, tn), lambda j: (0, j)),
        ],
        out_specs=pl.BlockSpec((depth, MOD_ROWS, tn), lambda j: (0, 0, j)),
        out_shape=jax.ShapeDtypeStruct((depth, MOD_ROWS, n), F32),
        compiler_params=_cparams(("arbitrary",), 48),
        name="modulation",
    )(c_pad, w_mod, b_mod.reshape(1, n), mod_offset.reshape(depth, n))
    return out.reshape(depth, MOD_ROWS, N_MOD, D)


def _tables_kernel(pos_ref, pat_ref, o_ref):
    pos = pos_ref[...]
    for t in range(3):
        ang = pos * pat_ref[t, 0:1, :]
        cos = jnp.cos(ang)
        sin = jnp.sin(ang)
        o_ref[t, 0] = cos * pat_ref[t, 1:2, :] + pat_ref[t, 2:3, :]
        o_ref[t, 1] = sin * pat_ref[t, 3:4, :]
        o_ref[t, 2] = sin * pat_ref[t, 4:5, :]


def _rope_pattern(head_dim, rot_dim, n_lanes):
    half = rot_dim // 2
    inv_freq = ROPE_THETA ** (-jnp.arange(half, dtype=F32) / half)
    d = np.arange(n_lanes) % head_dim
    in_rot = d < rot_dim
    invf = jnp.where(in_rot, inv_freq[d % half], 0.0)
    rows = [invf,
            jnp.asarray(in_rot, F32),
            jnp.asarray(~in_rot, F32),
            jnp.asarray(-(d < half).astype(np.float32)),
            jnp.asarray(((d >= half) & in_rot).astype(np.float32))]
    return jnp.stack(rows + [jnp.zeros((n_lanes,), F32)] * 3)


def _rope_tables(positions):
    M = positions.size
    attn = _rope_pattern(HEAD_DIM, ROT_DIM, LANES)
    idx = _rope_pattern(IDX_DIM, IDX_ROT_DIM, LANES)
    lane = np.arange(LANES)
    is_k = jnp.asarray(lane < IDX_DIM, F32)
    is_w = (lane >= IDX_DIM) & (lane < IDX_DIM + IDX_HEADS)
    w_scale = float(IDX_HEADS * IDX_DIM) ** -0.5
    kw = idx * is_k[None, :]
    kw = kw.at[2].set(idx[2] * is_k + jnp.asarray(is_w, F32) * w_scale)
    pat = jnp.stack([attn, idx, kw])
    tm = min(512, M)
    return pl.pallas_call(
        _tables_kernel,
        grid=(M // tm,),
        in_specs=[pl.BlockSpec((tm, 1), lambda i: (i, 0)),
                  pl.BlockSpec((3, 8, LANES), lambda i: (0, 0, 0))],
        out_specs=pl.BlockSpec((3, 3, tm, LANES), lambda i: (0, 0, i, 0)),
        out_shape=jax.ShapeDtypeStruct((3, 3, M, LANES), F32),
        compiler_params=_cparams(("parallel",), 32),
        name="rope_tables",
    )(positions.reshape(M, 1).astype(F32), pat)


NORM_ROWS = 64


def _prenorm_to(x_ref, dst_ref, m_ref, gpre_ref, sub):
    mul = gpre_ref[...] * (1.0 + m_ref[3 * sub + 1:3 * sub + 2, :])
    shift = m_ref[3 * sub:3 * sub + 1, :]

    def body(r, carry):
        rows = pl.ds(pl.multiple_of(r * NORM_ROWS, NORM_ROWS), NORM_ROWS)
        x = x_ref[rows, :]
        rs = lax.rsqrt(jnp.mean(x * x, axis=-1, keepdims=True) + RMS_EPS)
        dst_ref[rows, :] = (x_ref[rows, :] * rs * mul + shift).astype(BF16)
        return carry

    lax.fori_loop(0, x_ref.shape[0] // NORM_ROWS, body, 0)


def _postnorm_residual(x_ref, o_ref, m_ref, gpost_ref, sub, res):
    mul = (res * m_ref[3 * sub + 2:3 * sub + 3, :]) * gpost_ref[...]

    def body(r, carry):
        rows = pl.ds(pl.multiple_of(r * NORM_ROWS, NORM_ROWS), NORM_ROWS)
        y = o_ref[rows, :]
        rs = lax.rsqrt(jnp.mean(y * y, axis=-1, keepdims=True) + RMS_EPS)
        o_ref[rows, :] = x_ref[rows, :] + o_ref[rows, :] * rs * mul
        return carry

    lax.fori_loop(0, x_ref.shape[0] // NORM_ROWS, body, 0)


def _ffn_kernel(x_ref, m_ref, gpre_ref, gpost_ref, wg_ref, wu_ref, wd_ref, o_ref, xn_ref, *, sub):
    j = pl.program_id(1)

    @pl.when(j == 0)
    def _():
        _prenorm_to(x_ref, xn_ref, m_ref, gpre_ref, sub)
        o_ref[...] = jnp.zeros_like(o_ref)

    xn = xn_ref[...]
    h = _dot(xn, wg_ref[...])
    u = _dot(xn, wu_ref[...])
    o_ref[...] += _dot((h * jax.nn.sigmoid(h) * u).astype(BF16), wd_ref[...])

    @pl.when(j == pl.num_programs(1) - 1)
    def _():
        _postnorm_residual(x_ref, o_ref, m_ref, gpost_ref, sub, FFN_RES)


def _ffn(x, mods, l, sub, gpre, gpost, wg, wu, wd, S):
    M, D = x.shape
    F = wg.shape[-1]
    tm = min(512, S)
    tf = min(512, F)
    tpb = S // tm
    return pl.pallas_call(
        functools.partial(_ffn_kernel, sub=sub),
        grid=(M // tm, F // tf),
        in_specs=[
            pl.BlockSpec((tm, D), lambda i, j: (i, 0), pipeline_mode=pl.Buffered(1)),
            pl.BlockSpec((None, None, N_MOD, D), lambda i, j: (l, i // tpb, 0, 0)),
            pl.BlockSpec((1, D), lambda i, j: (0, 0)),
            pl.BlockSpec((1, D), lambda i, j: (0, 0)),
            pl.BlockSpec((None, D, tf), lambda i, j: (l, 0, j)),
            pl.BlockSpec((None, D, tf), lambda i, j: (l, 0, j)),
            pl.BlockSpec((None, tf, D), lambda i, j: (l, j, 0)),
        ],
        out_specs=pl.BlockSpec((tm, D), lambda i, j: (i, 0)),
        out_shape=jax.ShapeDtypeStruct((M, D), F32),
        scratch_shapes=[pltpu.VMEM((tm, D), BF16)],
        compiler_params=_cparams(("parallel", "arbitrary"), 60),
        name=f"ffn{sub}",
    )(x, mods, gpre.reshape(1, D), gpost.reshape(1, D), wg, wu, wd)


def _proj_kernel(x_ref, m_ref, gpre_ref, w_ref, o_ref, xn_ref):
    @pl.when(pl.program_id(1) == 0)
    def _():
        _prenorm_to(x_ref, xn_ref, m_ref, gpre_ref, 1)

    o_ref[...] = _dot_nt(xn_ref[...], w_ref[...]).astype(o_ref.dtype)


def _proj(x, mods, l, gpre, w, S, tn):
    M, D = x.shape
    N = w.shape[1]
    tm = min(1024, S)
    tpb = S // tm
    return pl.pallas_call(
        _proj_kernel,
        grid=(M // tm, N // tn),
        in_specs=[
            pl.BlockSpec((tm, D), lambda i, j: (i, 0), pipeline_mode=pl.Buffered(1)),
            pl.BlockSpec((None, None, N_MOD, D), lambda i, j: (l, i // tpb, 0, 0)),
            pl.BlockSpec((1, D), lambda i, j: (0, 0)),
            pl.BlockSpec((None, tn, D), lambda i, j: (l, j, 0)),
        ],
        out_specs=pl.BlockSpec((tm, tn), lambda i, j: (i, j)),
        out_shape=jax.ShapeDtypeStruct((M, N), BF16),
        scratch_shapes=[pltpu.VMEM((tm, D), BF16)],
        compiler_params=_cparams(("parallel", "arbitrary"), 60),
        name="mixer_proj",
    )(x, mods, gpre.reshape(1, D), w)


class _Layout:
    def __init__(self, D, lru_w, attn_w, kv_w, idx_w, pool_w, tn):
        split = (lru_w, lru_w, attn_w, kv_w, kv_w, idx_w, IDX_DIM, IDX_HEADS, pool_w, 3 * D)
        names = ("u_lru", "u_gate", "q", "k", "v", "q_idx", "k_idx", "w_idx", "u_pool", "g_br")
        src = dict(zip(names, zip(np.cumsum((0,) + split[:-1]).tolist(), split)))
        segs = [("u_lru", lru_w, ["u_lru"]), ("u_gate", lru_w, ["u_gate"]), ("q", attn_w, ["q"]),
                ("u_pool", pool_w, ["u_pool"]), ("g_br", D, ["g_br"]), ("k", kv_w, ["k"]),
                ("v", kv_w, ["v"]), ("q_idx", idx_w, ["q_idx"]), ("kw", LANES, ["k_idx", "w_idx"])]

        def place(order):
            off, pieces, pos = {}, [], 0
            for name, width, parts in order:
                if pos % width:
                    return None
                off[name] = pos
                length = 0
                for p in parts:
                    pieces.append(src[p])
                    length += src[p][1]
                padded = -(-length // width) * width
                if padded != length:
                    pieces.append((None, padded - length))
                pos += padded
            return off, pieces, pos

        self.off, pieces, pos = place(segs) or place(sorted(segs, key=lambda s: -s[1]))
        self.n = -(-pos // tn) * tn
        if self.n != pos:
            pieces.append((None, self.n - pos))
        self.pieces = []
        for start, length in pieces:
            last = self.pieces[-1] if self.pieces else None
            if last and (start is None) == (last[0] is None) and (start is None or last[0] + last[1] == start):
                self.pieces[-1] = (last[0], last[1] + length)
            else:
                self.pieces.append((start, length))
        self.d_in = sum(split)

    def pack_plan(self):
        spans, dest = [], 0
        for s, n in self.pieces:
            spans.append((dest, n, s))
            dest += n
        if self.n % PACK_BLOCK:
            return None
        plan = []
        for d0 in range(0, self.n, PACK_BLOCK):
            inside = [sp for sp in spans if sp[0] < d0 + PACK_BLOCK and sp[0] + sp[1] > d0]
            first, rest = inside[0], inside[1:]
            if any(sp[2] is not None for sp in rest):
                return None
            if first[2] is None:
                plan.append((0, 0))
                continue
            c = first[2] + d0 - first[0]
            if c % F32_SUBLANES or c + PACK_BLOCK > self.d_in:
                return None
            plan.append((c, min(PACK_BLOCK, first[0] + first[1] - d0)))
        return plan

    def pack_t(self, w_in):
        w_t = jnp.swapaxes(w_in, 1, 2)
        plan = self.pack_plan()
        if plan is not None:
            return _pack_rows(w_t, plan, self.n)
        rows = [jnp.zeros(w_t.shape[:1] + (n,) + w_t.shape[2:], BF16) if s is None else w_t[:, s:s + n].astype(BF16)
                for s, n in self.pieces]
        return jnp.concatenate(rows, axis=1)


PACK_BLOCK = 512
F32_SUBLANES = 8


def _pack_rows_kernel(src, nrows, w_ref, o_ref):
    keep = lax.broadcasted_iota(jnp.int32, (PACK_BLOCK, 1), 0) < nrows[pl.program_id(1)]
    o_ref[...] = jnp.where(keep, w_ref[0], 0.0).astype(BF16)


def _pack_rows(w_t, plan, n_out):
    L, _, K = w_t.shape
    src = jnp.asarray([p[0] for p in plan], jnp.int32)
    nrows = jnp.asarray([p[1] for p in plan], jnp.int32)
    grid_spec = pltpu.PrefetchScalarGridSpec(
        num_scalar_prefetch=2,
        grid=(L, n_out // PACK_BLOCK),
        in_specs=[pl.BlockSpec((pl.Element(1), pl.Element(PACK_BLOCK), pl.Element(K)),
                               lambda l, j, src, nrows: (l, pl.multiple_of(src[j], F32_SUBLANES), 0))],
        out_specs=pl.BlockSpec((None, PACK_BLOCK, K), lambda l, j, src, nrows: (l, j, 0)),
    )
    return pl.pallas_call(
        _pack_rows_kernel,
        grid_spec=grid_spec,
        out_shape=jax.ShapeDtypeStruct((L, n_out, K), BF16),
        compiler_params=_cparams(("parallel", "arbitrary"), 40),
        name="pack_w_in",
    )(src, nrows, w_t)


def _cast_kernel(x_ref, o_ref):
    o_ref[...] = x_ref[...].astype(o_ref.dtype)


def _cast_bf16(w):
    L, R, C = w.shape
    tr, tc = min(1024, R), min(2048, C)
    assert R % tr == 0 and C % tc == 0
    spec = pl.BlockSpec((None, tr, tc), lambda l, i, j: (l, i, j))
    return pl.pallas_call(
        _cast_kernel,
        grid=(L, R // tr, C // tc),
        in_specs=[spec],
        out_specs=spec,
        out_shape=jax.ShapeDtypeStruct((L, R, C), BF16),
        compiler_params=_cparams(("parallel", "parallel", "parallel"), 40),
        name="cast_bf16",
    )(w)


def _rope3(t, tab_ref, shift):
    return (t * tab_ref[0] + pltpu.roll(t, LANES - shift, 1) * tab_ref[1]
            + pltpu.roll(t, shift, 1) * tab_ref[2])


def _prep_kernel(q_ref, k_ref, qi_ref, kw_ref, tab_ref, qo_ref, ko_ref, qio_ref, kio_ref, wo_ref):
    scale = float(HEAD_DIM) ** -0.5
    for h in range(q_ref.shape[1] // LANES):
        sl = slice(h * LANES, (h + 1) * LANES)
        qo_ref[:, sl] = (_rope3(q_ref[:, sl].astype(F32), tab_ref.at[0], ROT_DIM // 2) * scale).astype(BF16)
    for h in range(k_ref.shape[1] // LANES):
        sl = slice(h * LANES, (h + 1) * LANES)
        ko_ref[:, sl] = _rope3(k_ref[:, sl].astype(F32), tab_ref.at[0], ROT_DIM // 2).astype(BF16)
    for h in range(qi_ref.shape[1] // LANES):
        sl = slice(h * LANES, (h + 1) * LANES)
        qio_ref[:, sl] = _rope3(qi_ref[:, sl].astype(F32), tab_ref.at[1], IDX_ROT_DIM // 2).astype(BF16)
    r = _rope3(kw_ref[...].astype(F32), tab_ref.at[2], IDX_ROT_DIM // 2)
    swapped = pltpu.roll(r, IDX_DIM, 1)
    lane = lax.broadcasted_iota(jnp.int32, r.shape, 1)
    kio_ref[...] = jnp.where(lane < IDX_DIM, r, swapped).astype(BF16)
    wo_ref[...] = swapped


def _prep(proj, tabs, lay, attn_w, kv_w, idx_w, S):
    M = proj.shape[0]
    tm = min(512, S)

    def col(name, width):
        c = lay.off[name] // width
        return pl.BlockSpec((tm, width), lambda i: (i, c))

    row = lambda width: pl.BlockSpec((tm, width), lambda i: (i, 0))
    return pl.pallas_call(
        _prep_kernel,
        grid=(M // tm,),
        in_specs=[col("q", attn_w), col("k", kv_w), col("q_idx", idx_w), col("kw", LANES),
                  pl.BlockSpec((3, 3, tm, LANES), lambda i: (0, 0, i, 0))],
        out_specs=[row(attn_w), row(kv_w), row(idx_w), row(LANES), row(LANES)],
        out_shape=[jax.ShapeDtypeStruct((M, attn_w), BF16), jax.ShapeDtypeStruct((M, kv_w), BF16),
                   jax.ShapeDtypeStruct((M, idx_w), BF16), jax.ShapeDtypeStruct((M, LANES), BF16),
                   jax.ShapeDtypeStruct((M, LANES), F32)],
        compiler_params=_cparams(("parallel",), 32),
        name="rope_prep",
    )(proj, proj, proj, proj, tabs)


LRU_HALO = 8
POOL_HALO = 16


def _gelu_tanh(x):
    return 0.5 * x * (1.0 + jnp.tanh(0.7978845608028654 * (x + 0.044715 * (x * x * x))))


def _seq_kernel(ul_ref, ug_ref, up_ref, cw_ref, cb_ref, wa_ref, ba_ref, wx_ref, bx_ref, lam_ref,
                pw_ref, ps_ref, lo_ref, po_ref, ext_l, ext_p, h_ref, a_s, b_s, *, ts):
    s = pl.program_id(1)

    @pl.when(s == 0)
    def _():
        ext_l[0:LRU_HALO, :] = jnp.zeros((LRU_HALO, ext_l.shape[1]), F32)
        ext_p[0:POOL_HALO, :] = jnp.zeros((POOL_HALO, ext_p.shape[1]), F32)
        h_ref[...] = jnp.zeros_like(h_ref)

    ext_l[LRU_HALO:LRU_HALO + ts, :] = ul_ref[...].astype(F32)
    nblk, bw = wa_ref.shape[0], wa_ref.shape[1]
    for h in range(nblk):
        sl = slice(h * bw, (h + 1) * bw)
        xc = cb_ref[:, sl]
        for j in range(CONV_WIDTH):
            xc = xc + cw_ref[j:j + 1, sl] * ext_l[pl.ds(LRU_HALO - (CONV_WIDTH - 1) + j, ts), sl]
        xb = xc.astype(BF16)
        r = jax.nn.sigmoid(_dot(xb, wa_ref[h]) + ba_ref[:, sl])
        i = jax.nn.sigmoid(_dot(xb, wx_ref[h]) + bx_ref[:, sl])
        nl = -lam_ref[:, sl]
        softplus = jnp.maximum(nl, 0.0) + jnp.log1p(jnp.exp(-jnp.abs(nl)))
        a = jnp.exp((-LRU_C) * r * softplus)
        a_s[:, sl] = a
        b_s[:, sl] = jnp.sqrt(1.0 - a * a) * (i * xc)
    ext_l[0:LRU_HALO, :] = ext_l[ts:ts + LRU_HALO, :]

    def step(t, h):
        h = a_s[pl.ds(t, 1), :] * h + b_s[pl.ds(t, 1), :]
        b_s[pl.ds(t, 1), :] = h
        return h

    h_ref[...] = lax.fori_loop(0, ts, step, h_ref[...], unroll=8)
    lo_ref[...] = (b_s[...] * _gelu_tanh(ug_ref[...].astype(F32))).astype(BF16)

    ext_p[POOL_HALO:POOL_HALO + ts, :] = up_ref[...].astype(F32)
    pg = pw_ref.shape[1]
    t_glob = s * ts + lax.broadcasted_iota(jnp.int32, (ts, 1), 0)
    for g, w in enumerate(POOL_WINDOWS):
        sl = slice(g * pg, (g + 1) * pg)
        tot = ext_p[POOL_HALO:POOL_HALO + ts, sl]
        cur = tot
        for j in range(1, w):
            tot = tot + ext_p[pl.ds(POOL_HALO - j, ts), sl]
        cnt = jnp.minimum(t_glob + 1, w).astype(F32)
        pooled = (tot / cnt - cur).astype(BF16)
        po_ref[:, sl] = (_dot(pooled, pw_ref[g]) * ps_ref[:, sl]).astype(BF16)
    ext_p[0:POOL_HALO, :] = ext_p[ts:ts + POOL_HALO, :]


def _seq(proj, lay, B, S, conv_w, conv_b, w_a, b_a, w_x, b_x, lam, pool_w, pool_scale):
    M = proj.shape[0]
    lw = conv_w.shape[-1]
    pw = pool_scale.shape[-1]
    ts = min(256, S)
    nt = S // ts

    def col(name, width):
        c = lay.off[name] // width
        return pl.BlockSpec((ts, width), lambda b, s: (b * nt + s, c))

    full = lambda a: pl.BlockSpec(a.shape, lambda b, s: (0,) * a.ndim)
    small = [conv_w, conv_b.reshape(1, lw), w_a, b_a.reshape(1, lw), w_x, b_x.reshape(1, lw),
             lam.reshape(1, lw), pool_w, pool_scale.reshape(1, pw)]
    return pl.pallas_call(
        functools.partial(_seq_kernel, ts=ts),
        grid=(B, nt),
        in_specs=[col("u_lru", lw), col("u_gate", lw), col("u_pool", pw)] + [full(a) for a in small],
        out_specs=[pl.BlockSpec((ts, lw), lambda b, s: (b * nt + s, 0)),
                   pl.BlockSpec((ts, pw), lambda b, s: (b * nt + s, 0))],
        out_shape=[jax.ShapeDtypeStruct((M, lw), BF16), jax.ShapeDtypeStruct((M, pw), BF16)],
        scratch_shapes=[pltpu.VMEM((LRU_HALO + ts, lw), F32), pltpu.VMEM((POOL_HALO + ts, pw), F32),
                        pltpu.VMEM((1, lw), F32), pltpu.VMEM((ts, lw), F32), pltpu.VMEM((ts, lw), F32)],
        compiler_params=_cparams(("arbitrary", "arbitrary"), 48),
        name="lru_pool",
    )(proj, proj, proj, *small)


COUNT_ROWS = 64


def _attn_kernel(q_ref, qi_ref, w_ref, k_ref, v_ref, ki_ref, o_ref, keys_ref, qs_ref, m_ref, acc_ref,
                 *, tq, tk, topk):
    qb = pl.program_id(1)
    n_chunks = ((qb + 1) * tq + tk - 1) // tk
    row_pos = qb * tq + lax.broadcasted_iota(jnp.int32, (tq, 1), 0)
    low_half = lax.broadcasted_iota(jnp.int32, (1, LANES), 1) < IDX_DIM
    high_half = lax.broadcasted_iota(jnp.int32, (1, LANES), 1) >= IDX_DIM
    n_pairs = qi_ref.shape[1] // LANES

    def score_chunk(c, carry):
        start = pl.multiple_of(c * tk, tk)
        kic = ki_ref[pl.ds(start, tk), :]
        acc = jnp.zeros((tq, tk), F32)
        for p in range(n_pairs):
            qp = qi_ref[:, p * LANES:(p + 1) * LANES]
            for half in range(2):
                qm = jnp.where(low_half if half == 0 else high_half, qp, jnp.zeros_like(qp))
                hd = 2 * p + half
                acc = acc + w_ref[:, hd:hd + 1] * jnp.maximum(_dot_nt(qm, kic), 0.0)
        bits = lax.bitcast_convert_type(acc, jnp.int32)
        key = bits ^ ((bits >> 31) & 0x7FFFFFFF)
        col_pos = start + lax.broadcasted_iota(jnp.int32, (1, tk), 1)
        keys_ref[c] = jnp.where(col_pos <= row_pos, key, INT_MIN)
        return carry

    lax.fori_loop(0, n_chunks, score_chunk, 0)

    def bisect(it, thr):
        cand = thr + jnp.left_shift(jnp.int32(1), 31 - it)

        parts = []
        for rb in range(tq // COUNT_ROWS):
            rows = slice(rb * COUNT_ROWS, (rb + 1) * COUNT_ROWS)
            cand_b = jnp.broadcast_to(cand[rows], (COUNT_ROWS, LANES))

            def count_chunk(c, acc, rows=rows, cand_b=cand_b):
                for u in range(tk // LANES):
                    acc = acc + jnp.where(keys_ref[c, rows, u * LANES:(u + 1) * LANES] >= cand_b, 1.0, 0.0)
                return acc

            parts.append(lax.fori_loop(0, n_chunks, count_chunk, jnp.zeros((COUNT_ROWS, LANES), F32)))
        cnt = jnp.sum(jnp.concatenate(parts, axis=0), axis=-1, keepdims=True)
        return jnp.where(cnt >= float(topk), cand, thr)

    thr = lax.fori_loop(0, 32, bisect, jnp.full((tq, 1), INT_MIN, jnp.int32))
    thr = jnp.maximum(thr, INT_MIN + 1)

    def selection_bias(c):
        return jnp.where(keys_ref[c] >= thr, 0.0, MASK_NEG)

    n_heads = q_ref.shape[1] // HEAD_DIM
    group = n_heads // N_KV_HEADS
    gr = group * tq
    n_sub = tk // LANES
    for h in range(n_heads):
        qs_ref[h * tq:(h + 1) * tq, :] = q_ref[:, h * HEAD_DIM:(h + 1) * HEAD_DIM]

    def masked_logits(c, bias, g):
        start = pl.multiple_of(c * tk, tk)
        logits = _dot_nt(qs_ref[g * gr:(g + 1) * gr, :], k_ref[pl.ds(start, tk), g * HEAD_DIM:(g + 1) * HEAD_DIM])
        return (logits.reshape(group, tq, tk) + bias[None]).reshape(gr, tk)

    m_ref[...] = jnp.full(m_ref.shape, MASK_NEG, F32)

    def max_chunk(c, carry):
        bias = selection_bias(c)
        for g in range(N_KV_HEADS):
            logits = masked_logits(c, bias, g)
            m_part = m_ref[g * gr:(g + 1) * gr, :]
            for u in range(n_sub):
                m_part = jnp.maximum(m_part, logits[:, u * LANES:(u + 1) * LANES])
            m_ref[g * gr:(g + 1) * gr, :] = m_part
        return carry

    lax.fori_loop(0, n_chunks, max_chunk, 0)
    for g in range(N_KV_HEADS):
        rows = slice(g * gr, (g + 1) * gr)
        m_ref[rows, :] = jnp.broadcast_to(jnp.max(m_ref[rows, :], axis=-1, keepdims=True), (gr, LANES))

    acc_ref[...] = jnp.zeros(acc_ref.shape, F32)

    def acc_chunk(c, carry):
        start = pl.multiple_of(c * tk, tk)
        bias = selection_bias(c)
        ones = jnp.ones((tk, HEAD_DIM), BF16)
        for g in range(N_KV_HEADS):
            rows = slice(g * gr, (g + 1) * gr)
            logits = masked_logits(c, bias, g)
            m_row = m_ref[rows, :]
            p = jnp.concatenate([jnp.exp(logits[:, u * LANES:(u + 1) * LANES] - m_row) for u in range(n_sub)],
                                axis=1).astype(BF16)
            v_ext = jnp.concatenate([v_ref[pl.ds(start, tk), g * HEAD_DIM:(g + 1) * HEAD_DIM], ones], axis=1)
            acc_ref[rows, :] += _dot(p, v_ext)
        return carry

    lax.fori_loop(0, n_chunks, acc_chunk, 0)
    for h in range(n_heads):
        rows = slice(h * tq, (h + 1) * tq)
        o_ref[:, h * HEAD_DIM:(h + 1) * HEAD_DIM] = (
            acc_ref[rows, :HEAD_DIM] / acc_ref[rows, HEAD_DIM:HEAD_DIM + 1]).astype(BF16)


def _attention(q, qi, w, k, proj, ki, lay, B, S):
    M, attn_w = q.shape
    kv_w = k.shape[1]
    idx_w = qi.shape[1]
    tq = min(512, S)
    tk = min(512, S)
    nq = S // tq
    topk = min(TOPK_MAX, S // 4)
    v_col = lay.off["v"] // kv_w
    return pl.pallas_call(
        functools.partial(_attn_kernel, tq=tq, tk=tk, topk=topk),
        grid=(B, nq),
        in_specs=[
            pl.BlockSpec((tq, attn_w), lambda b, i: (b * nq + i, 0)),
            pl.BlockSpec((tq, idx_w), lambda b, i: (b * nq + i, 0)),
            pl.BlockSpec((tq, LANES), lambda b, i: (b * nq + i, 0)),
            pl.BlockSpec((S, kv_w), lambda b, i: (b, 0)),
            pl.BlockSpec((S, kv_w), lambda b, i: (b, v_col)),
            pl.BlockSpec((S, LANES), lambda b, i: (b, 0)),
        ],
        out_specs=pl.BlockSpec((tq, attn_w), lambda b, i: (b * nq + i, 0)),
        out_shape=jax.ShapeDtypeStruct((M, attn_w), BF16),
        scratch_shapes=[pltpu.VMEM((S // tk, tq, tk), jnp.int32),
                        pltpu.VMEM((attn_w // HEAD_DIM * tq, HEAD_DIM), BF16),
                        pltpu.VMEM((attn_w // HEAD_DIM * tq, LANES), F32),
                        pltpu.VMEM((attn_w // HEAD_DIM * tq, 2 * HEAD_DIM), F32)],
        compiler_params=_cparams(("parallel", "arbitrary"), 56),
        name="indexer_attention",
    )(q, qi, w, k, proj, ki)


def _branch_kernel(al_ref, aa_ref, ap_ref, g0_ref, g1_ref, g2_ref, wl_ref, wa_ref, wp_ref, o_ref):
    merged = (jax.nn.sigmoid(g0_ref[...].astype(F32)) * _dot(al_ref[...], wl_ref[...])
              + jax.nn.sigmoid(g1_ref[...].astype(F32)) * _dot(aa_ref[...], wa_ref[...])
              + jax.nn.sigmoid(g2_ref[...].astype(F32)) * _dot(ap_ref[...], wp_ref[...]))
    o_ref[...] = merged.astype(o_ref.dtype)


def _branch_merge(l, a_lru, a_attn, a_pool, proj, lay, w_l, w_a, w_p, S):
    M = a_lru.shape[0]
    D = w_l.shape[-1]
    tm = min(1024, S)
    tn = min(512, D)
    g_base = lay.off["g_br"] // tn
    nj = D // tn

    def gspec(br):
        return pl.BlockSpec((tm, tn), lambda i, j: (i, g_base + br * nj + j))

    def aspec(a):
        return pl.BlockSpec((tm, a.shape[1]), lambda i, j: (i, 0), pipeline_mode=pl.Buffered(1))

    def wspec(w):
        return pl.BlockSpec((None, w.shape[1], tn), lambda i, j: (l, 0, j))

    return pl.pallas_call(
        _branch_kernel,
        grid=(M // tm, nj),
        in_specs=[aspec(a_lru), aspec(a_attn), aspec(a_pool), gspec(0), gspec(1), gspec(2),
                  wspec(w_l), wspec(w_a), wspec(w_p)],
        out_specs=pl.BlockSpec((tm, tn), lambda i, j: (i, j)),
        out_shape=jax.ShapeDtypeStruct((M, D), BF16),
        compiler_params=_cparams(("parallel", "arbitrary"), 56),
        name="branch_merge",
    )(a_lru, a_attn, a_pool, proj, proj, proj, w_l, w_a, w_p)


def _outproj_kernel(x_ref, m_ref, gpost_ref, a_ref, w_ref, o_ref):
    j = pl.program_id(1)

    @pl.when(j == 0)
    def _():
        o_ref[...] = jnp.zeros_like(o_ref)

    o_ref[...] += _dot(a_ref[...], w_ref[...])

    @pl.when(j == pl.num_programs(1) - 1)
    def _():
        _postnorm_residual(x_ref, o_ref, m_ref, gpost_ref, 1, 1.0)


def _outproj(x, mods, l, gpost, merged, w_o, S):
    M, D = x.shape
    K = merged.shape[1]
    tm = min(512, S)
    tk = min(1024, K)
    tpb = S // tm
    return pl.pallas_call(
        _outproj_kernel,
        grid=(M // tm, K // tk),
        in_specs=[
            pl.BlockSpec((tm, D), lambda i, j: (i, 0), pipeline_mode=pl.Buffered(1)),
            pl.BlockSpec((None, None, N_MOD, D), lambda i, j: (l, i // tpb, 0, 0)),
            pl.BlockSpec((1, D), lambda i, j: (0, 0)),
            pl.BlockSpec((tm, tk), lambda i, j: (i, j)),
            pl.BlockSpec((None, tk, D), lambda i, j: (l, j, 0)),
        ],
        out_specs=pl.BlockSpec((tm, D), lambda i, j: (i, 0)),
        out_shape=jax.ShapeDtypeStruct((M, D), F32),
        compiler_params=_cparams(("parallel", "arbitrary"), 56),
        name="out_proj",
    )(x, mods, gpost.reshape(1, D), merged, w_o)


def kernel(x, c, positions, w_mod, b_mod, mod_offset, norm_pre, norm_post, ffn1_w_gate, ffn1_w_up, ffn1_w_down, w_in, conv_w, conv_b, lru_w_a, lru_b_a, lru_w_x, lru_b_x, lru_lambda, pool_w, pool_scale, w_br_lru, w_br_attn, w_br_pool, w_out, ffn2_w_gate, ffn2_w_up, ffn2_w_down):
    B, S, D = x.shape
    M = B * S
    depth = w_in.shape[0]
    lru_w = conv_w.shape[-1]
    attn_w = w_br_attn.shape[1]
    kv_w = N_KV_HEADS * HEAD_DIM
    idx_w = IDX_HEADS * IDX_DIM
    pool_wd = pool_scale.shape[-1]
    tn_proj = 768
    lay = _Layout(D, lru_w, attn_w, kv_w, idx_w, pool_wd, tn_proj)
    assert lay.d_in == w_in.shape[-1]
    assert B <= MOD_ROWS and S % LANES == 0

    bf = _cast_bf16
    w_in_p = lay.pack_t(w_in)
    f1g, f1u, f1d = bf(ffn1_w_gate), bf(ffn1_w_up), bf(ffn1_w_down)
    f2g, f2u, f2d = bf(ffn2_w_gate), bf(ffn2_w_up), bf(ffn2_w_down)
    wbl, wba, wbp, wo = bf(w_br_lru), bf(w_br_attn), bf(w_br_pool), bf(w_out)
    wa, wx, pw = lru_w_a.astype(BF16), lru_w_x.astype(BF16), pool_w.astype(BF16)

    mods = _modulation(c, w_mod, b_mod, mod_offset)
    tabs = _rope_tables(positions)

    xf = x.reshape(M, D)
    for l in range(depth):
        xf = _ffn(xf, mods, l, 0, norm_pre[l, 0], norm_post[l, 0], f1g, f1u, f1d, S)
        proj = _proj(xf, mods, l, norm_pre[l, 1], w_in_p, S, tn_proj)
        q, k, qi, ki, w = _prep(proj, tabs, lay, attn_w, kv_w, idx_w, S)
        a_lru, a_pool = _seq(proj, lay, B, S, conv_w[l], conv_b[l], wa[l], lru_b_a[l], wx[l], lru_b_x[l],
                             lru_lambda[l], pw[l], pool_scale[l])
        a_attn = _attention(q, qi, w, k, proj, ki, lay, B, S)
        merged = _branch_merge(l, a_lru, a_attn, a_pool, proj, lay, wbl, wba, wbp, S)
        xf = _outproj(xf, mods, l, norm_post[l, 1], merged, wo, S)
        xf = _ffn(xf, mods, l, 2, norm_pre[l, 2], norm_post[l, 2], f2g, f2u, f2d, S)
    return xf.reshape(B, S, D)
```

```python
import functools

import numpy as np
import jax
import jax.numpy as jnp
from jax import lax
from jax.experimental import pallas as pl
from jax.experimental.pallas import tpu as pltpu

F32 = jnp.float32
BF16 = jnp.bfloat16

N_MOD = 9
FFN_RES = 0.5
RMS_EPS = 1e-6
CONV_WIDTH = 4
LRU_C = 8.0
HEAD_DIM = 128
N_KV_HEADS = 4
ROT_DIM = HEAD_DIM // 4
IDX_HEADS = 16
IDX_DIM = 64
IDX_ROT_DIM = IDX_DIM // 4
TOPK_MAX = 256
ROPE_THETA = 500000.0
POOL_WINDOWS = (2, 4, 8, 16)

LANES = 128
MOD_ROWS = 8
MIB = 1024 * 1024
INT_MIN = -2 ** 31
HALF_MIN = -2 ** 15
MASK_NEG = -1e30


def _cparams(semantics, vmem_mib):
    return pltpu.CompilerParams(dimension_semantics=semantics, vmem_limit_bytes=vmem_mib * MIB)


def _dot(a, b):
    return jnp.dot(a, b, preferred_element_type=F32)


def _dot_nt(a, b):
    return lax.dot_general(a, b, (((1,), (1,)), ((), ())), preferred_element_type=F32)


def _mod_kernel(c_ref, w_ref, b_ref, off_ref, o_ref):
    c = c_ref[...]
    s = (c * jax.nn.sigmoid(c)).astype(BF16)
    base = _dot(s, w_ref[...].astype(BF16)) + b_ref[...]
    for l in range(o_ref.shape[0]):
        o_ref[l] = base + off_ref[l:l + 1, :]


def _modulation(c, w_mod, b_mod, mod_offset):
    B, D = c.shape
    depth = mod_offset.shape[0]
    n = w_mod.shape[1]
    tn = min(1024, n)
    c_pad = jnp.zeros((MOD_ROWS, D), F32).at[:B].set(c)
    out = pl.pallas_call(
        _mod_kernel,
        grid=(n // tn,),
        in_specs=[
            pl.BlockSpec((MOD_ROWS, D), lambda j: (0, 0)),
            pl.BlockSpec((D, tn), lambda j: (0, j)),
            pl.BlockSpec((1, tn), lambda j: (0, j)),
            pl.BlockSpec((depth, tn), lambda j: (0, j)),
        ],
        out_specs=pl.BlockSpec((depth, MOD_ROWS, tn), lambda j: (0, 0, j)),
        out_shape=jax.ShapeDtypeStruct((depth, MOD_ROWS, n), F32),
        compiler_params=_cparams(("arbitrary",), 48),
        name="modulation",
    )(c_pad, w_mod, b_mod.reshape(1, n), mod_offset.reshape(depth, n))
    return out.reshape(depth, MOD_ROWS, N_MOD, D)


def _tables_kernel(pos_ref, pat_ref, o_ref):
    pos = pos_ref[...]
    for t in range(3):
        ang = pos * pat_ref[t, 0:1, :]
        cos = jnp.cos(ang)
        sin = jnp.sin(ang)
        o_ref[t, 0] = cos * pat_ref[t, 1:2, :] + pat_ref[t, 2:3, :]
        o_ref[t, 1] = sin * pat_ref[t, 3:4, :]
        o_ref[t, 2] = sin * pat_ref[t, 4:5, :]


def _rope_pattern(head_dim, rot_dim, n_lanes):
    half = rot_dim // 2
    inv_freq = ROPE_THETA ** (-jnp.arange(half, dtype=F32) / half)
    d = np.arange(n_lanes) % head_dim
    in_rot = d < rot_dim
    invf = jnp.where(in_rot, inv_freq[d % half], 0.0)
    rows = [invf,
            jnp.asarray(in_rot, F32),
            jnp.asarray(~in_rot, F32),
            jnp.asarray(-(d < half).astype(np.float32)),
            jnp.asarray(((d >= half) & in_rot).astype(np.float32))]
    return jnp.stack(rows + [jnp.zeros((n_lanes,), F32)] * 3)


def _rope_tables(positions):
    M = positions.size
    attn = _rope_pattern(HEAD_DIM, ROT_DIM, LANES)
    idx = _rope_pattern(IDX_DIM, IDX_ROT_DIM, LANES)
    lane = np.arange(LANES)
    is_k = jnp.asarray(lane < IDX_DIM, F32)
    is_w = (lane >= IDX_DIM) & (lane < IDX_DIM + IDX_HEADS)
    w_scale = float(IDX_HEADS * IDX_DIM) ** -0.5
    kw = idx * is_k[None, :]
    kw = kw.at[2].set(idx[2] * is_k + jnp.asarray(is_w, F32) * w_scale)
    pat = jnp.stack([attn, idx, kw])
    tm = min(512, M)
    return pl.pallas_call(
        _tables_kernel,
        grid=(M // tm,),
        in_specs=[pl.BlockSpec((tm, 1), lambda i: (i, 0)),
                  pl.BlockSpec((3, 8, LANES), lambda i: (0, 0, 0))],
        out_specs=pl.BlockSpec((3, 3, tm, LANES), lambda i: (0, 0, i, 0)),
        out_shape=jax.ShapeDtypeStruct((3, 3, M, LANES), F32),
        compiler_params=_cparams(("parallel",), 32),
        name="rope_tables",
    )(positions.reshape(M, 1).astype(F32), pat)


NORM_ROWS = 64


def _prenorm_to(x_ref, dst_ref, m_ref, gpre_ref, sub):
    mul = gpre_ref[...] * (1.0 + m_ref[3 * sub + 1:3 * sub + 2, :])
    shift = m_ref[3 * sub:3 * sub + 1, :]

    def body(r, carry):
        rows = pl.ds(pl.multiple_of(r * NORM_ROWS, NORM_ROWS), NORM_ROWS)
        x = x_ref[rows, :]
        rs = lax.rsqrt(jnp.mean(x * x, axis=-1, keepdims=True) + RMS_EPS)
        dst_ref[rows, :] = (x_ref[rows, :] * rs * mul + shift).astype(BF16)
        return carry

    lax.fori_loop(0, x_ref.shape[0] // NORM_ROWS, body, 0)


def _postnorm_residual(x_ref, o_ref, m_ref, gpost_ref, sub, res):
    mul = (res * m_ref[3 * sub + 2:3 * sub + 3, :]) * gpost_ref[...]

    def body(r, carry):
        rows = pl.ds(pl.multiple_of(r * NORM_ROWS, NORM_ROWS), NORM_ROWS)
        y = o_ref[rows, :]
        rs = lax.rsqrt(jnp.mean(y * y, axis=-1, keepdims=True) + RMS_EPS)
        o_ref[rows, :] = x_ref[rows, :] + o_ref[rows, :] * rs * mul
        return carry

    lax.fori_loop(0, x_ref.shape[0] // NORM_ROWS, body, 0)


def _ffn_kernel(x_ref, m_ref, gpre_ref, gpost_ref, wg_ref, wu_ref, wd_ref, o_ref, xn_ref, *, sub):
    j = pl.program_id(1)

    @pl.when(j == 0)
    def _():
        _prenorm_to(x_ref, xn_ref, m_ref, gpre_ref, sub)
        o_ref[...] = jnp.zeros_like(o_ref)

    xn = xn_ref[...]
    h = _dot(xn, wg_ref[...])
    u = _dot(xn, wu_ref[...])
    o_ref[...] += _dot((h * jax.nn.sigmoid(h) * u).astype(BF16), wd_ref[...])

    @pl.when(j == pl.num_programs(1) - 1)
    def _():
        _postnorm_residual(x_ref, o_ref, m_ref, gpost_ref, sub, FFN_RES)


def _ffn(x, mods, l, sub, gpre, gpost, wg, wu, wd, S):
    M, D = x.shape
    F = wg.shape[-1]
    tm = min(512, S)
    tf = min(512, F)
    tpb = S // tm
    return pl.pallas_call(
        functools.partial(_ffn_kernel, sub=sub),
        grid=(M // tm, F // tf),
        in_specs=[
            pl.BlockSpec((tm, D), lambda i, j: (i, 0), pipeline_mode=pl.Buffered(1)),
            pl.BlockSpec((None, None, N_MOD, D), lambda i, j: (l, i // tpb, 0, 0)),
            pl.BlockSpec((1, D), lambda i, j: (0, 0)),
            pl.BlockSpec((1, D), lambda i, j: (0, 0)),
            pl.BlockSpec((None, D, tf), lambda i, j: (l, 0, j)),
            pl.BlockSpec((None, D, tf), lambda i, j: (l, 0, j)),
            pl.BlockSpec((None, tf, D), lambda i, j: (l, j, 0)),
        ],
        out_specs=pl.BlockSpec((tm, D), lambda i, j: (i, 0)),
        out_shape=jax.ShapeDtypeStruct((M, D), F32),
        scratch_shapes=[pltpu.VMEM((tm, D), BF16)],
        compiler_params=_cparams(("parallel", "arbitrary"), 60),
        name=f"ffn{sub}",
    )(x, mods, gpre.reshape(1, D), gpost.reshape(1, D), wg, wu, wd)


def _proj_kernel(x_ref, m_ref, gpre_ref, w_ref, o_ref, xn_ref):
    @pl.when(pl.program_id(1) == 0)
    def _():
        _prenorm_to(x_ref, xn_ref, m_ref, gpre_ref, 1)

    o_ref[...] = _dot_nt(xn_ref[...], w_ref[...]).astype(o_ref.dtype)


def _proj(x, mods, l, gpre, w, S, tn):
    M, D = x.shape
    N = w.shape[1]
    tm = min(1024, S)
    tpb = S // tm
    return pl.pallas_call(
        _proj_kernel,
        grid=(M // tm, N // tn),
        in_specs=[
            pl.BlockSpec((tm, D), lambda i, j: (i, 0), pipeline_mode=pl.Buffered(1)),
            pl.BlockSpec((None, None, N_MOD, D), lambda i, j: (l, i // tpb, 0, 0)),
            pl.BlockSpec((1, D), lambda i, j: (0, 0)),
            pl.BlockSpec((None, tn, D), lambda i, j: (l, j, 0)),
        ],
        out_specs=pl.BlockSpec((tm, tn), lambda i, j: (i, j)),
        out_shape=jax.ShapeDtypeStruct((M, N), BF16),
        scratch_shapes=[pltpu.VMEM((tm, D), BF16)],
        compiler_params=_cparams(("parallel", "arbitrary"), 60),
        name="mixer_proj",
    )(x, mods, gpre.reshape(1, D), w)


class _Layout:
    def __init__(self, D, lru_w, attn_w, kv_w, idx_w, pool_w, tn):
        split = (lru_w, lru_w, attn_w, kv_w, kv_w, idx_w, IDX_DIM, IDX_HEADS, pool_w, 3 * D)
        names = ("u_lru", "u_gate", "q", "k", "v", "q_idx", "k_idx", "w_idx", "u_pool", "g_br")
        src = dict(zip(names, zip(np.cumsum((0,) + split[:-1]).tolist(), split)))
        segs = [("u_lru", lru_w, ["u_lru"]), ("u_gate", lru_w, ["u_gate"]), ("q", attn_w, ["q"]),
                ("u_pool", pool_w, ["u_pool"]), ("g_br", D, ["g_br"]), ("k", kv_w, ["k"]),
                ("v", kv_w, ["v"]), ("q_idx", idx_w, ["q_idx"]), ("kw", LANES, ["k_idx", "w_idx"])]

        def place(order):
            off, pieces, pos = {}, [], 0
            for name, width, parts in order:
                if pos % width:
                    return None
                off[name] = pos
                length = 0
                for p in parts:
                    pieces.append(src[p])
                    length += src[p][1]
                padded = -(-length // width) * width
                if padded != length:
                    pieces.append((None, padded - length))
                pos += padded
            return off, pieces, pos

        self.off, pieces, pos = place(segs) or place(sorted(segs, key=lambda s: -s[1]))
        self.n = -(-pos // tn) * tn
        if self.n != pos:
            pieces.append((None, self.n - pos))
        self.pieces = []
        for start, length in pieces:
            last = self.pieces[-1] if self.pieces else None
            if last and (start is None) == (last[0] is None) and (start is None or last[0] + last[1] == start):
                self.pieces[-1] = (last[0], last[1] + length)
            else:
                self.pieces.append((start, length))
        self.d_in = sum(split)

    def pack_plan(self):
        spans, dest = [], 0
        for s, n in self.pieces:
            spans.append((dest, n, s))
            dest += n
        if self.n % PACK_BLOCK:
            return None
        plan = []
        for d0 in range(0, self.n, PACK_BLOCK):
            inside = [sp for sp in spans if sp[0] < d0 + PACK_BLOCK and sp[0] + sp[1] > d0]
            first, rest = inside[0], inside[1:]
            if any(sp[2] is not None for sp in rest):
                return None
            if first[2] is None:
                plan.append((0, 0))
                continue
            c = first[2] + d0 - first[0]
            if c % F32_SUBLANES or c + PACK_BLOCK > self.d_in:
                return None
            plan.append((c, min(PACK_BLOCK, first[0] + first[1] - d0)))
        return plan

    def pack_t(self, w_in):
        w_t = jnp.swapaxes(w_in, 1, 2)
        plan = self.pack_plan()
        if plan is not None:
            return _pack_rows(w_t, plan, self.n)
        rows = [jnp.zeros(w_t.shape[:1] + (n,) + w_t.shape[2:], BF16) if s is None else w_t[:, s:s + n].astype(BF16)
                for s, n in self.pieces]
        return jnp.concatenate(rows, axis=1)


PACK_BLOCK = 512
F32_SUBLANES = 8


def _pack_rows_kernel(src, nrows, w_ref, o_ref):
    keep = lax.broadcasted_iota(jnp.int32, (PACK_BLOCK, 1), 0) < nrows[pl.program_id(1)]
    o_ref[...] = jnp.where(keep, w_ref[0], 0.0).astype(BF16)


def _pack_rows(w_t, plan, n_out):
    L, _, K = w_t.shape
    src = jnp.asarray([p[0] for p in plan], jnp.int32)
    nrows = jnp.asarray([p[1] for p in plan], jnp.int32)
    grid_spec = pltpu.PrefetchScalarGridSpec(
        num_scalar_prefetch=2,
        grid=(L, n_out // PACK_BLOCK),
        in_specs=[pl.BlockSpec((pl.Element(1), pl.Element(PACK_BLOCK), pl.Element(K)),
                               lambda l, j, src, nrows: (l, pl.multiple_of(src[j], F32_SUBLANES), 0))],
        out_specs=pl.BlockSpec((None, PACK_BLOCK, K), lambda l, j, src, nrows: (l, j, 0)),
    )
    return pl.pallas_call(
        _pack_rows_kernel,
        grid_spec=grid_spec,
        out_shape=jax.ShapeDtypeStruct((L, n_out, K), BF16),
        compiler_params=_cparams(("parallel", "arbitrary"), 40),
        name="pack_w_in",
    )(src, nrows, w_t)


def _cast_kernel(x_ref, o_ref):
    o_ref[...] = x_ref[...].astype(o_ref.dtype)


def _cast_bf16(w):
    L, R, C = w.shape
    tr, tc = min(1024, R), min(2048, C)
    assert R % tr == 0 and C % tc == 0
    spec = pl.BlockSpec((None, tr, tc), lambda l, i, j: (l, i, j))
    return pl.pallas_call(
        _cast_kernel,
        grid=(L, R // tr, C // tc),
        in_specs=[spec],
        out_specs=spec,
        out_shape=jax.ShapeDtypeStruct((L, R, C), BF16),
        compiler_params=_cparams(("parallel", "parallel", "parallel"), 40),
        name="cast_bf16",
    )(w)


def _rope3(t, tab_ref, shift):
    return (t * tab_ref[0] + pltpu.roll(t, LANES - shift, 1) * tab_ref[1]
            + pltpu.roll(t, shift, 1) * tab_ref[2])


def _prep_kernel(q_ref, k_ref, qi_ref, kw_ref, tab_ref, qo_ref, ko_ref, qio_ref, kio_ref, wo_ref):
    scale = float(HEAD_DIM) ** -0.5
    for h in range(q_ref.shape[1] // LANES):
        sl = slice(h * LANES, (h + 1) * LANES)
        qo_ref[:, sl] = (_rope3(q_ref[:, sl].astype(F32), tab_ref.at[0], ROT_DIM // 2) * scale).astype(BF16)
    for h in range(k_ref.shape[1] // LANES):
        sl = slice(h * LANES, (h + 1) * LANES)
        ko_ref[:, sl] = _rope3(k_ref[:, sl].astype(F32), tab_ref.at[0], ROT_DIM // 2).astype(BF16)
    for h in range(qi_ref.shape[1] // LANES):
        sl = slice(h * LANES, (h + 1) * LANES)
        qio_ref[:, sl] = _rope3(qi_ref[:, sl].astype(F32), tab_ref.at[1], IDX_ROT_DIM // 2).astype(BF16)
    r = _rope3(kw_ref[...].astype(F32), tab_ref.at[2], IDX_ROT_DIM // 2)
    swapped = pltpu.roll(r, IDX_DIM, 1)
    lane = lax.broadcasted_iota(jnp.int32, r.shape, 1)
    kio_ref[...] = jnp.where(lane < IDX_DIM, r, swapped).astype(BF16)
    wo_ref[...] = swapped


def _prep(proj, tabs, lay, attn_w, kv_w, idx_w, S):
    M = proj.shape[0]
    tm = min(512, S)

    def col(name, width):
        c = lay.off[name] // width
        return pl.BlockSpec((tm, width), lambda i: (i, c))

    row = lambda width: pl.BlockSpec((tm, width), lambda i: (i, 0))
    return pl.pallas_call(
        _prep_kernel,
        grid=(M // tm,),
        in_specs=[col("q", attn_w), col("k", kv_w), col("q_idx", idx_w), col("kw", LANES),
                  pl.BlockSpec((3, 3, tm, LANES), lambda i: (0, 0, i, 0))],
        out_specs=[row(attn_w), row(kv_w), row(idx_w), row(LANES), row(LANES)],
        out_shape=[jax.ShapeDtypeStruct((M, attn_w), BF16), jax.ShapeDtypeStruct((M, kv_w), BF16),
                   jax.ShapeDtypeStruct((M, idx_w), BF16), jax.ShapeDtypeStruct((M, LANES), BF16),
                   jax.ShapeDtypeStruct((M, LANES), F32)],
        compiler_params=_cparams(("parallel",), 32),
        name="rope_prep",
    )(proj, proj, proj, proj, tabs)


LRU_HALO = 8
POOL_HALO = 16


def _gelu_tanh(x):
    return 0.5 * x * (1.0 + jnp.tanh(0.7978845608028654 * (x + 0.044715 * (x * x * x))))


def _seq_kernel(ul_ref, ug_ref, up_ref, cw_ref, cb_ref, wa_ref, ba_ref, wx_ref, bx_ref, lam_ref,
                pw_ref, ps_ref, lo_ref, po_ref, ext_l, ext_p, h_ref, a_s, b_s, *, ts):
    s = pl.program_id(1)

    @pl.when(s == 0)
    def _():
        ext_l[0:LRU_HALO, :] = jnp.zeros((LRU_HALO, ext_l.shape[1]), F32)
        ext_p[0:POOL_HALO, :] = jnp.zeros((POOL_HALO, ext_p.shape[1]), F32)
        h_ref[...] = jnp.zeros_like(h_ref)

    ext_l[LRU_HALO:LRU_HALO + ts, :] = ul_ref[...].astype(F32)
    nblk, bw = wa_ref.shape[0], wa_ref.shape[1]
    for h in range(nblk):
        sl = slice(h * bw, (h + 1) * bw)
        xc = cb_ref[:, sl]
        for j in range(CONV_WIDTH):
            xc = xc + cw_ref[j:j + 1, sl] * ext_l[pl.ds(LRU_HALO - (CONV_WIDTH - 1) + j, ts), sl]
        xb = xc.astype(BF16)
        r = jax.nn.sigmoid(_dot(xb, wa_ref[h]) + ba_ref[:, sl])
        i = jax.nn.sigmoid(_dot(xb, wx_ref[h]) + bx_ref[:, sl])
        nl = -lam_ref[:, sl]
        softplus = jnp.maximum(nl, 0.0) + jnp.log1p(jnp.exp(-jnp.abs(nl)))
        a = jnp.exp((-LRU_C) * r * softplus)
        a_s[:, sl] = a
        b_s[:, sl] = jnp.sqrt(1.0 - a * a) * (i * xc)
    ext_l[0:LRU_HALO, :] = ext_l[ts:ts + LRU_HALO, :]

    def step(t, h):
        h = a_s[pl.ds(t, 1), :] * h + b_s[pl.ds(t, 1), :]
        b_s[pl.ds(t, 1), :] = h
        return h

    h_ref[...] = lax.fori_loop(0, ts, step, h_ref[...], unroll=8)
    lo_ref[...] = (b_s[...] * _gelu_tanh(ug_ref[...].astype(F32))).astype(BF16)

    ext_p[POOL_HALO:POOL_HALO + ts, :] = up_ref[...].astype(F32)
    pg = pw_ref.shape[1]
    t_glob = s * ts + lax.broadcasted_iota(jnp.int32, (ts, 1), 0)
    for g, w in enumerate(POOL_WINDOWS):
        sl = slice(g * pg, (g + 1) * pg)
        tot = ext_p[POOL_HALO:POOL_HALO + ts, sl]
        cur = tot
        for j in range(1, w):
            tot = tot + ext_p[pl.ds(POOL_HALO - j, ts), sl]
        cnt = jnp.minimum(t_glob + 1, w).astype(F32)
        pooled = (tot / cnt - cur).astype(BF16)
        po_ref[:, sl] = (_dot(pooled, pw_ref[g]) * ps_ref[:, sl]).astype(BF16)
    ext_p[0:POOL_HALO, :] = ext_p[ts:ts + POOL_HALO, :]


def _seq(proj, lay, B, S, conv_w, conv_b, w_a, b_a, w_x, b_x, lam, pool_w, pool_scale):
    M = proj.shape[0]
    lw = conv_w.shape[-1]
    pw = pool_scale.shape[-1]
    ts = min(256, S)
    nt = S // ts

    def col(name, width):
        c = lay.off[name] // width
        return pl.BlockSpec((ts, width), lambda b, s: (b * nt + s, c))

    full = lambda a: pl.BlockSpec(a.shape, lambda b, s: (0,) * a.ndim)
    small = [conv_w, conv_b.reshape(1, lw), w_a, b_a.reshape(1, lw), w_x, b_x.reshape(1, lw),
             lam.reshape(1, lw), pool_w, pool_scale.reshape(1, pw)]
    return pl.pallas_call(
        functools.partial(_seq_kernel, ts=ts),
        grid=(B, nt),
        in_specs=[col("u_lru", lw), col("u_gate", lw), col("u_pool", pw)] + [full(a) for a in small],
        out_specs=[pl.BlockSpec((ts, lw), lambda b, s: (b * nt + s, 0)),
                   pl.BlockSpec((ts, pw), lambda b, s: (b * nt + s, 0))],
        out_shape=[jax.ShapeDtypeStruct((M, lw), BF16), jax.ShapeDtypeStruct((M, pw), BF16)],
        scratch_shapes=[pltpu.VMEM((LRU_HALO + ts, lw), F32), pltpu.VMEM((POOL_HALO + ts, pw), F32),
                        pltpu.VMEM((1, lw), F32), pltpu.VMEM((ts, lw), F32), pltpu.VMEM((ts, lw), F32)],
        compiler_params=_cparams(("arbitrary", "arbitrary"), 48),
        name="lru_pool",
    )(proj, proj, proj, *small)


COUNT_ROWS = 128


def _attn_kernel(q_ref, qi_ref, w_ref, k_ref, v_ref, ki_ref, o_ref, keys_ref, half_ref, qs_ref, m_ref, acc_ref,
                 *, tq, tk, topk):
    qb = pl.program_id(1)
    n_chunks = ((qb + 1) * tq + tk - 1) // tk
    row_pos = qb * tq + lax.broadcasted_iota(jnp.int32, (tq, 1), 0)
    low_half = lax.broadcasted_iota(jnp.int32, (1, LANES), 1) < IDX_DIM
    high_half = lax.broadcasted_iota(jnp.int32, (1, LANES), 1) >= IDX_DIM
    n_pairs = qi_ref.shape[1] // LANES

    def score_chunk(c, carry):
        start = pl.multiple_of(c * tk, tk)
        kic = ki_ref[pl.ds(start, tk), :]
        acc = jnp.zeros((tq, tk), F32)
        for p in range(n_pairs):
            qp = qi_ref[:, p * LANES:(p + 1) * LANES]
            for half in range(2):
                qm = jnp.where(low_half if half == 0 else high_half, qp, jnp.zeros_like(qp))
                hd = 2 * p + half
                acc = acc + w_ref[:, hd:hd + 1] * jnp.maximum(_dot_nt(qm, kic), 0.0)
        bits = lax.bitcast_convert_type(acc, jnp.int32)
        key = bits ^ ((bits >> 31) & 0x7FFFFFFF)
        col_pos = start + lax.broadcasted_iota(jnp.int32, (1, tk), 1)
        key = jnp.where(col_pos <= row_pos, key, INT_MIN)
        keys_ref[c] = key
        half_ref[c] = (key >> 16).astype(jnp.int16)
        return carry

    lax.fori_loop(0, n_chunks, score_chunk, 0)

    def count_ge(cand):
        parts = []
        for rb in range(tq // COUNT_ROWS):
            rows = slice(rb * COUNT_ROWS, (rb + 1) * COUNT_ROWS)
            cand_b = jnp.broadcast_to(cand[rows], (COUNT_ROWS, LANES)).astype(jnp.int16)

            def count_chunk(c, acc, rows=rows, cand_b=cand_b):
                for u in range(tk // LANES):
                    hit = half_ref[c, rows, u * LANES:(u + 1) * LANES] >= cand_b
                    acc = acc + jnp.where(hit, jnp.int16(1), jnp.int16(0))
                return acc

            parts.append(lax.fori_loop(0, n_chunks, count_chunk, jnp.zeros((COUNT_ROWS, LANES), jnp.int16)))
        return jnp.sum(jnp.concatenate(parts, axis=0).astype(F32), axis=-1, keepdims=True)

    def bisect16(rank):
        def step(it, thr):
            cand = thr + jnp.left_shift(jnp.int32(1), 15 - it)
            return jnp.where(count_ge(cand) >= rank, cand, thr)

        return lax.fori_loop(0, 16, step, jnp.full((tq, 1), HALF_MIN, jnp.int32))

    thr_hi = bisect16(float(topk))
    above = count_ge(jnp.minimum(thr_hi + 1, -HALF_MIN - 1))

    def fill_low(c, carry):
        key = keys_ref[c]
        low = (key & 0xFFFF) + HALF_MIN
        half_ref[c] = jnp.where((key >> 16) == thr_hi, low, HALF_MIN).astype(jnp.int16)
        return carry

    lax.fori_loop(0, n_chunks, fill_low, 0)
    thr_lo = bisect16(float(topk) - above)
    thr = thr_hi * 65536 + (thr_lo - HALF_MIN)
    thr = jnp.maximum(thr, INT_MIN + 1)

    def selection_bias(c):
        return jnp.where(keys_ref[c] >= thr, 0.0, MASK_NEG)

    n_heads = q_ref.shape[1] // HEAD_DIM
    group = n_heads // N_KV_HEADS
    gr = group * tq
    n_sub = tk // LANES
    for h in range(n_heads):
        qs_ref[h * tq:(h + 1) * tq, :] = q_ref[:, h * HEAD_DIM:(h + 1) * HEAD_DIM]

    def masked_logits(c, bias, g):
        start = pl.multiple_of(c * tk, tk)
        logits = _dot_nt(qs_ref[g * gr:(g + 1) * gr, :], k_ref[pl.ds(start, tk), g * HEAD_DIM:(g + 1) * HEAD_DIM])
        return (logits.reshape(group, tq, tk) + bias[None]).reshape(gr, tk)

    m_ref[...] = jnp.full(m_ref.shape, MASK_NEG, F32)

    def max_chunk(c, carry):
        bias = selection_bias(c)
        for g in range(N_KV_HEADS):
            logits = masked_logits(c, bias, g)
            m_part = m_ref[g * gr:(g + 1) * gr, :]
            for u in range(n_sub):
                m_part = jnp.maximum(m_part, logits[:, u * LANES:(u + 1) * LANES])
            m_ref[g * gr:(g + 1) * gr, :] = m_part
        return carry

    lax.fori_loop(0, n_chunks, max_chunk, 0)
    for g in range(N_KV_HEADS):
        rows = slice(g * gr, (g + 1) * gr)
        m_ref[rows, :] = jnp.broadcast_to(jnp.max(m_ref[rows, :], axis=-1, keepdims=True), (gr, LANES))

    acc_ref[...] = jnp.zeros(acc_ref.shape, F32)

    def acc_chunk(c, carry):
        start = pl.multiple_of(c * tk, tk)
        bias = selection_bias(c)
        ones = jnp.ones((tk, HEAD_DIM), BF16)
        for g in range(N_KV_HEADS):
            rows = slice(g * gr, (g + 1) * gr)
            logits = masked_logits(c, bias, g)
            m_row = m_ref[rows, :]
            p = jnp.concatenate([jnp.exp(logits[:, u * LANES:(u + 1) * LANES] - m_row) for u in range(n_sub)],
                                axis=1).astype(BF16)
            v_ext = jnp.concatenate([v_ref[pl.ds(start, tk), g * HEAD_DIM:(g + 1) * HEAD_DIM], ones], axis=1)
            acc_ref[rows, :] += _dot(p, v_ext)
        return carry

    lax.fori_loop(0, n_chunks, acc_chunk, 0)
    for h in range(n_heads):
        rows = slice(h * tq, (h + 1) * tq)
        o_ref[:, h * HEAD_DIM:(h + 1) * HEAD_DIM] = (
            acc_ref[rows, :HEAD_DIM] / acc_ref[rows, HEAD_DIM:HEAD_DIM + 1]).astype(BF16)


def _attention(q, qi, w, k, proj, ki, lay, B, S):
    M, attn_w = q.shape
    kv_w = k.shape[1]
    idx_w = qi.shape[1]
    tq = min(512, S)
    tk = min(512, S)
    nq = S // tq
    topk = min(TOPK_MAX, S // 4)
    v_col = lay.off["v"] // kv_w
    return pl.pallas_call(
        functools.partial(_attn_kernel, tq=tq, tk=tk, topk=topk),
        grid=(B, nq),
        in_specs=[
            pl.BlockSpec((tq, attn_w), lambda b, i: (b * nq + i, 0)),
            pl.BlockSpec((tq, idx_w), lambda b, i: (b * nq + i, 0)),
            pl.BlockSpec((tq, LANES), lambda b, i: (b * nq + i, 0)),
            pl.BlockSpec((S, kv_w), lambda b, i: (b, 0), pipeline_mode=pl.Buffered(1)),
            pl.BlockSpec((S, kv_w), lambda b, i: (b, v_col), pipeline_mode=pl.Buffered(1)),
            pl.BlockSpec((S, LANES), lambda b, i: (b, 0), pipeline_mode=pl.Buffered(1)),
        ],
        out_specs=pl.BlockSpec((tq, attn_w), lambda b, i: (b * nq + i, 0)),
        out_shape=jax.ShapeDtypeStruct((M, attn_w), BF16),
        scratch_shapes=[pltpu.VMEM((S // tk, tq, tk), jnp.int32),
                        pltpu.VMEM((S // tk, tq, tk), jnp.int16),
                        pltpu.VMEM((attn_w // HEAD_DIM * tq, HEAD_DIM), BF16),
                        pltpu.VMEM((attn_w // HEAD_DIM * tq, LANES), F32),
                        pltpu.VMEM((attn_w // HEAD_DIM * tq, 2 * HEAD_DIM), F32)],
        compiler_params=_cparams(("parallel", "arbitrary"), 60),
        name="indexer_attention",
    )(q, qi, w, k, proj, ki)


def _branch_kernel(al_ref, aa_ref, ap_ref, g0_ref, g1_ref, g2_ref, wl_ref, wa_ref, wp_ref, o_ref):
    merged = (jax.nn.sigmoid(g0_ref[...].astype(F32)) * _dot(al_ref[...], wl_ref[...])
              + jax.nn.sigmoid(g1_ref[...].astype(F32)) * _dot(aa_ref[...], wa_ref[...])
              + jax.nn.sigmoid(g2_ref[...].astype(F32)) * _dot(ap_ref[...], wp_ref[...]))
    o_ref[...] = merged.astype(o_ref.dtype)


def _branch_merge(l, a_lru, a_attn, a_pool, proj, lay, w_l, w_a, w_p, S):
    M = a_lru.shape[0]
    D = w_l.shape[-1]
    tm = min(1024, S)
    tn = min(512, D)
    g_base = lay.off["g_br"] // tn
    nj = D // tn

    def gspec(br):
        return pl.BlockSpec((tm, tn), lambda i, j: (i, g_base + br * nj + j))

    def aspec(a):
        return pl.BlockSpec((tm, a.shape[1]), lambda i, j: (i, 0), pipeline_mode=pl.Buffered(1))

    def wspec(w):
        return pl.BlockSpec((None, w.shape[1], tn), lambda i, j: (l, 0, j))

    return pl.pallas_call(
        _branch_kernel,
        grid=(M // tm, nj),
        in_specs=[aspec(a_lru), aspec(a_attn), aspec(a_pool), gspec(0), gspec(1), gspec(2),
                  wspec(w_l), wspec(w_a), wspec(w_p)],
        out_specs=pl.BlockSpec((tm, tn), lambda i, j: (i, j)),
        out_shape=jax.ShapeDtypeStruct((M, D), BF16),
        compiler_params=_cparams(("parallel", "arbitrary"), 56),
        name="branch_merge",
    )(a_lru, a_attn, a_pool, proj, proj, proj, w_l, w_a, w_p)


def _outproj_kernel(x_ref, m_ref, gpost_ref, a_ref, w_ref, o_ref):
    j = pl.program_id(1)

    @pl.when(j == 0)
    def _():
        o_ref[...] = jnp.zeros_like(o_ref)

    o_ref[...] += _dot(a_ref[...], w_ref[...])

    @pl.when(j == pl.num_programs(1) - 1)
    def _():
        _postnorm_residual(x_ref, o_ref, m_ref, gpost_ref, 1, 1.0)


def _outproj(x, mods, l, gpost, merged, w_o, S):
    M, D = x.shape
    K = merged.shape[1]
    tm = min(512, S)
    tk = min(1024, K)
    tpb = S // tm
    return pl.pallas_call(
        _outproj_kernel,
        grid=(M // tm, K // tk),
        in_specs=[
            pl.BlockSpec((tm, D), lambda i, j: (i, 0), pipeline_mode=pl.Buffered(1)),
            pl.BlockSpec((None, None, N_MOD, D), lambda i, j: (l, i // tpb, 0, 0)),
            pl.BlockSpec((1, D), lambda i, j: (0, 0)),
            pl.BlockSpec((tm, tk), lambda i, j: (i, j)),
            pl.BlockSpec((None, tk, D), lambda i, j: (l, j, 0)),
        ],
        out_specs=pl.BlockSpec((tm, D), lambda i, j: (i, 0)),
        out_shape=jax.ShapeDtypeStruct((M, D), F32),
        compiler_params=_cparams(("parallel", "arbitrary"), 56),
        name="out_proj",
    )(x, mods, gpost.reshape(1, D), merged, w_o)


def kernel(x, c, positions, w_mod, b_mod, mod_offset, norm_pre, norm_post, ffn1_w_gate, ffn1_w_up, ffn1_w_down, w_in, conv_w, conv_b, lru_w_a, lru_b_a, lru_w_x, lru_b_x, lru_lambda, pool_w, pool_scale, w_br_lru, w_br_attn, w_br_pool, w_out, ffn2_w_gate, ffn2_w_up, ffn2_w_down):
    B, S, D = x.shape
    M = B * S
    depth = w_in.shape[0]
    lru_w = conv_w.shape[-1]
    attn_w = w_br_attn.shape[1]
    kv_w = N_KV_HEADS * HEAD_DIM
    idx_w = IDX_HEADS * IDX_DIM
    pool_wd = pool_scale.shape[-1]
    tn_proj = 768
    lay = _Layout(D, lru_w, attn_w, kv_w, idx_w, pool_wd, tn_proj)
    assert lay.d_in == w_in.shape[-1]
    assert B <= MOD_ROWS and S % LANES == 0

    bf = _cast_bf16
    w_in_p = lay.pack_t(w_in)
    f1g, f1u, f1d = bf(ffn1_w_gate), bf(ffn1_w_up), bf(ffn1_w_down)
    f2g, f2u, f2d = bf(ffn2_w_gate), bf(ffn2_w_up), bf(ffn2_w_down)
    wbl, wba, wbp, wo = bf(w_br_lru), bf(w_br_attn), bf(w_br_pool), bf(w_out)
    wa, wx, pw = lru_w_a.astype(BF16), lru_w_x.astype(BF16), pool_w.astype(BF16)

    mods = _modulation(c, w_mod, b_mod, mod_offset)
    tabs = _rope_tables(positions)

    xf = x.reshape(M, D)
    for l in range(depth):
        xf = _ffn(xf, mods, l, 0, norm_pre[l, 0], norm_post[l, 0], f1g, f1u, f1d, S)
        proj = _proj(xf, mods, l, norm_pre[l, 1], w_in_p, S, tn_proj)
        q, k, qi, ki, w = _prep(proj, tabs, lay, attn_w, kv_w, idx_w, S)
        a_lru, a_pool = _seq(proj, lay, B, S, conv_w[l], conv_b[l], wa[l], lru_b_a[l], wx[l], lru_b_x[l],
                             lru_lambda[l], pw[l], pool_scale[l])
        a_attn = _attention(q, qi, w, k, proj, ki, lay, B, S)
        merged = _branch_merge(l, a_lru, a_attn, a_pool, proj, lay, wbl, wba, wbp, S)
        xf = _outproj(xf, mods, l, norm_post[l, 1], merged, wo, S)
        xf = _ffn(xf, mods, l, 2, norm_pre[l, 2], norm_post[l, 2], f2g, f2u, f2d, S)
    return xf.reshape(B, S, D)
```

```python
import functools

import numpy as np
import jax
import jax.numpy as jnp
from jax import lax
from jax.experimental import pallas as pl
from jax.experimental.pallas import tpu as pltpu

F32 = jnp.float32
BF16 = jnp.bfloat16

N_MOD = 9
FFN_RES = 0.5
RMS_EPS = 1e-6
CONV_WIDTH = 4
LRU_C = 8.0
HEAD_DIM = 128
N_KV_HEADS = 4
ROT_DIM = HEAD_DIM // 4
IDX_HEADS = 16
IDX_DIM = 64
IDX_ROT_DIM = IDX_DIM // 4
TOPK_MAX = 256
ROPE_THETA = 500000.0
POOL_WINDOWS = (2, 4, 8, 16)

LANES = 128
MOD_ROWS = 8
MIB = 1024 * 1024
INT_MIN = -2 ** 31
INT_MAX = 2 ** 31 - 1
MASK_NEG = -1e30


def _cparams(semantics, vmem_mib):
    return pltpu.CompilerParams(dimension_semantics=semantics, vmem_limit_bytes=vmem_mib * MIB)


def _dot(a, b):
    return jnp.dot(a, b, preferred_element_type=F32)


def _dot_nt(a, b):
    return lax.dot_general(a, b, (((1,), (1,)), ((), ())), preferred_element_type=F32)


def _mod_kernel(c_ref, w_ref, b_ref, off_ref, o_ref):
    c = c_ref[...]
    s = (c * jax.nn.sigmoid(c)).astype(BF16)
    base = _dot(s, w_ref[...].astype(BF16)) + b_ref[...]
    for l in range(o_ref.shape[0]):
        o_ref[l] = base + off_ref[l:l + 1, :]


def _modulation(c, w_mod, b_mod, mod_offset):
    B, D = c.shape
    depth = mod_offset.shape[0]
    n = w_mod.shape[1]
    tn = min(1024, n)
    c_pad = jnp.zeros((MOD_ROWS, D), F32).at[:B].set(c)
    out = pl.pallas_call(
        _mod_kernel,
        grid=(n // tn,),
        in_specs=[
            pl.BlockSpec((MOD_ROWS, D), lambda j: (0, 0)),
            pl.BlockSpec((D, tn), lambda j: (0, j)),
            pl.BlockSpec((1, tn), lambda j: (0, j)),
            pl.BlockSpec((depth, tn), lambda j: (0, j)),
        ],
        out_specs=pl.BlockSpec((depth, MOD_ROWS, tn), lambda j: (0, 0, j)),
        out_shape=jax.ShapeDtypeStruct((depth, MOD_ROWS, n), F32),
        compiler_params=_cparams(("arbitrary",), 48),
        name="modulation",
    )(c_pad, w_mod, b_mod.reshape(1, n), mod_offset.reshape(depth, n))
    return out.reshape(depth, MOD_ROWS, N_MOD, D)


def _tables_kernel(pos_ref, pat_ref, o_ref):
    pos = pos_ref[...]
    for t in range(3):
        ang = pos * pat_ref[t, 0:1, :]
        cos = jnp.cos(ang)
        sin = jnp.sin(ang)
        o_ref[t, 0] = cos * pat_ref[t, 1:2, :] + pat_ref[t, 2:3, :]
        o_ref[t, 1] = sin * pat_ref[t, 3:4, :]
        o_ref[t, 2] = sin * pat_ref[t, 4:5, :]


def _rope_pattern(head_dim, rot_dim, n_lanes):
    half = rot_dim // 2
    inv_freq = ROPE_THETA ** (-jnp.arange(half, dtype=F32) / half)
    d = np.arange(n_lanes) % head_dim
    in_rot = d < rot_dim
    invf = jnp.where(in_rot, inv_freq[d % half], 0.0)
    rows = [invf,
            jnp.asarray(in_rot, F32),
            jnp.asarray(~in_rot, F32),
            jnp.asarray(-(d < half).astype(np.float32)),
            jnp.asarray(((d >= half) & in_rot).astype(np.float32))]
    return jnp.stack(rows + [jnp.zeros((n_lanes,), F32)] * 3)


def _rope_tables(positions):
    M = positions.size
    attn = _rope_pattern(HEAD_DIM, ROT_DIM, LANES)
    idx = _rope_pattern(IDX_DIM, IDX_ROT_DIM, LANES)
    lane = np.arange(LANES)
    is_k = jnp.asarray(lane < IDX_DIM, F32)
    is_w = (lane >= IDX_DIM) & (lane < IDX_DIM + IDX_HEADS)
    w_scale = float(IDX_HEADS * IDX_DIM) ** -0.5
    kw = idx * is_k[None, :]
    kw = kw.at[2].set(idx[2] * is_k + jnp.asarray(is_w, F32) * w_scale)
    pat = jnp.stack([attn, idx, kw])
    tm = min(512, M)
    return pl.pallas_call(
        _tables_kernel,
        grid=(M // tm,),
        in_specs=[pl.BlockSpec((tm, 1), lambda i: (i, 0)),
                  pl.BlockSpec((3, 8, LANES), lambda i: (0, 0, 0))],
        out_specs=pl.BlockSpec((3, 3, tm, LANES), lambda i: (0, 0, i, 0)),
        out_shape=jax.ShapeDtypeStruct((3, 3, M, LANES), F32),
        compiler_params=_cparams(("parallel",), 32),
        name="rope_tables",
    )(positions.reshape(M, 1).astype(F32), pat)


NORM_ROWS = 64


def _prenorm_to(x_ref, dst_ref, m_ref, gpre_ref, sub):
    mul = gpre_ref[...] * (1.0 + m_ref[3 * sub + 1:3 * sub + 2, :])
    shift = m_ref[3 * sub:3 * sub + 1, :]

    def body(r, carry):
        rows = pl.ds(pl.multiple_of(r * NORM_ROWS, NORM_ROWS), NORM_ROWS)
        x = x_ref[rows, :]
        rs = lax.rsqrt(jnp.mean(x * x, axis=-1, keepdims=True) + RMS_EPS)
        dst_ref[rows, :] = (x_ref[rows, :] * rs * mul + shift).astype(BF16)
        return carry

    lax.fori_loop(0, x_ref.shape[0] // NORM_ROWS, body, 0)


def _postnorm_residual(x_ref, o_ref, m_ref, gpost_ref, sub, res):
    mul = (res * m_ref[3 * sub + 2:3 * sub + 3, :]) * gpost_ref[...]

    def body(r, carry):
        rows = pl.ds(pl.multiple_of(r * NORM_ROWS, NORM_ROWS), NORM_ROWS)
        y = o_ref[rows, :]
        rs = lax.rsqrt(jnp.mean(y * y, axis=-1, keepdims=True) + RMS_EPS)
        o_ref[rows, :] = x_ref[rows, :] + o_ref[rows, :] * rs * mul
        return carry

    lax.fori_loop(0, x_ref.shape[0] // NORM_ROWS, body, 0)


def _ffn_kernel(x_ref, m_ref, gpre_ref, gpost_ref, wg_ref, wu_ref, wd_ref, o_ref, xn_ref, *, sub):
    j = pl.program_id(1)

    @pl.when(j == 0)
    def _():
        _prenorm_to(x_ref, xn_ref, m_ref, gpre_ref, sub)
        o_ref[...] = jnp.zeros_like(o_ref)

    xn = xn_ref[...]
    h = _dot(xn, wg_ref[...])
    u = _dot(xn, wu_ref[...])
    o_ref[...] += _dot((h * jax.nn.sigmoid(h) * u).astype(BF16), wd_ref[...])

    @pl.when(j == pl.num_programs(1) - 1)
    def _():
        _postnorm_residual(x_ref, o_ref, m_ref, gpost_ref, sub, FFN_RES)


def _ffn(x, mods, l, sub, gpre, gpost, wg, wu, wd, S):
    M, D = x.shape
    F = wg.shape[-1]
    tm = min(512, S)
    tf = min(512, F)
    tpb = S // tm
    return pl.pallas_call(
        functools.partial(_ffn_kernel, sub=sub),
        grid=(M // tm, F // tf),
        in_specs=[
            pl.BlockSpec((tm, D), lambda i, j: (i, 0), pipeline_mode=pl.Buffered(1)),
            pl.BlockSpec((None, None, N_MOD, D), lambda i, j: (l, i // tpb, 0, 0)),
            pl.BlockSpec((1, D), lambda i, j: (0, 0)),
            pl.BlockSpec((1, D), lambda i, j: (0, 0)),
            pl.BlockSpec((None, D, tf), lambda i, j: (l, 0, j)),
            pl.BlockSpec((None, D, tf), lambda i, j: (l, 0, j)),
            pl.BlockSpec((None, tf, D), lambda i, j: (l, j, 0)),
        ],
        out_specs=pl.BlockSpec((tm, D), lambda i, j: (i, 0)),
        out_shape=jax.ShapeDtypeStruct((M, D), F32),
        scratch_shapes=[pltpu.VMEM((tm, D), BF16)],
        compiler_params=_cparams(("parallel", "arbitrary"), 60),
        name=f"ffn{sub}",
    )(x, mods, gpre.reshape(1, D), gpost.reshape(1, D), wg, wu, wd)


def _proj_kernel(x_ref, m_ref, gpre_ref, w_ref, o_ref, xn_ref):
    @pl.when(pl.program_id(1) == 0)
    def _():
        _prenorm_to(x_ref, xn_ref, m_ref, gpre_ref, 1)

    o_ref[...] = _dot_nt(xn_ref[...], w_ref[...]).astype(o_ref.dtype)


def _proj(x, mods, l, gpre, w, S, tn):
    M, D = x.shape
    N = w.shape[1]
    tm = min(1024, S)
    tpb = S // tm
    return pl.pallas_call(
        _proj_kernel,
        grid=(M // tm, N // tn),
        in_specs=[
            pl.BlockSpec((tm, D), lambda i, j: (i, 0), pipeline_mode=pl.Buffered(1)),
            pl.BlockSpec((None, None, N_MOD, D), lambda i, j: (l, i // tpb, 0, 0)),
            pl.BlockSpec((1, D), lambda i, j: (0, 0)),
            pl.BlockSpec((None, tn, D), lambda i, j: (l, j, 0)),
        ],
        out_specs=pl.BlockSpec((tm, tn), lambda i, j: (i, j)),
        out_shape=jax.ShapeDtypeStruct((M, N), BF16),
        scratch_shapes=[pltpu.VMEM((tm, D), BF16)],
        compiler_params=_cparams(("parallel", "arbitrary"), 60),
        name="mixer_proj",
    )(x, mods, gpre.reshape(1, D), w)


class _Layout:
    def __init__(self, D, lru_w, attn_w, kv_w, idx_w, pool_w, tn):
        split = (lru_w, lru_w, attn_w, kv_w, kv_w, idx_w, IDX_DIM, IDX_HEADS, pool_w, 3 * D)
        names = ("u_lru", "u_gate", "q", "k", "v", "q_idx", "k_idx", "w_idx", "u_pool", "g_br")
        src = dict(zip(names, zip(np.cumsum((0,) + split[:-1]).tolist(), split)))
        segs = [("u_lru", lru_w, ["u_lru"]), ("u_gate", lru_w, ["u_gate"]), ("q", attn_w, ["q"]),
                ("u_pool", pool_w, ["u_pool"]), ("g_br", D, ["g_br"]), ("k", kv_w, ["k"]),
                ("v", kv_w, ["v"]), ("q_idx", idx_w, ["q_idx"]), ("kw", LANES, ["k_idx", "w_idx"])]

        def place(order):
            off, pieces, pos = {}, [], 0
            for name, width, parts in order:
                if pos % width:
                    return None
                off[name] = pos
                length = 0
                for p in parts:
                    pieces.append(src[p])
                    length += src[p][1]
                padded = -(-length // width) * width
                if padded != length:
                    pieces.append((None, padded - length))
                pos += padded
            return off, pieces, pos

        self.off, pieces, pos = place(segs) or place(sorted(segs, key=lambda s: -s[1]))
        self.n = -(-pos // tn) * tn
        if self.n != pos:
            pieces.append((None, self.n - pos))
        self.pieces = []
        for start, length in pieces:
            last = self.pieces[-1] if self.pieces else None
            if last and (start is None) == (last[0] is None) and (start is None or last[0] + last[1] == start):
                self.pieces[-1] = (last[0], last[1] + length)
            else:
                self.pieces.append((start, length))
        self.d_in = sum(split)

    def pack_plan(self):
        spans, dest = [], 0
        for s, n in self.pieces:
            spans.append((dest, n, s))
            dest += n
        if self.n % PACK_BLOCK:
            return None
        plan = []
        for d0 in range(0, self.n, PACK_BLOCK):
            inside = [sp for sp in spans if sp[0] < d0 + PACK_BLOCK and sp[0] + sp[1] > d0]
            first, rest = inside[0], inside[1:]
            if any(sp[2] is not None for sp in rest):
                return None
            if first[2] is None:
                plan.append((0, 0))
                continue
            c = first[2] + d0 - first[0]
            if c % F32_SUBLANES or c + PACK_BLOCK > self.d_in:
                return None
            plan.append((c, min(PACK_BLOCK, first[0] + first[1] - d0)))
        return plan

    def pack_t(self, w_in):
        w_t = jnp.swapaxes(w_in, 1, 2)
        plan = self.pack_plan()
        if plan is not None:
            return _pack_rows(w_t, plan, self.n)
        rows = [jnp.zeros(w_t.shape[:1] + (n,) + w_t.shape[2:], BF16) if s is None else w_t[:, s:s + n].astype(BF16)
                for s, n in self.pieces]
        return jnp.concatenate(rows, axis=1)


PACK_BLOCK = 512
F32_SUBLANES = 8


def _pack_rows_kernel(src, nrows, w_ref, o_ref):
    keep = lax.broadcasted_iota(jnp.int32, (PACK_BLOCK, 1), 0) < nrows[pl.program_id(1)]
    o_ref[...] = jnp.where(keep, w_ref[0], 0.0).astype(BF16)


def _pack_rows(w_t, plan, n_out):
    L, _, K = w_t.shape
    src = jnp.asarray([p[0] for p in plan], jnp.int32)
    nrows = jnp.asarray([p[1] for p in plan], jnp.int32)
    grid_spec = pltpu.PrefetchScalarGridSpec(
        num_scalar_prefetch=2,
        grid=(L, n_out // PACK_BLOCK),
        in_specs=[pl.BlockSpec((pl.Element(1), pl.Element(PACK_BLOCK), pl.Element(K)),
                               lambda l, j, src, nrows: (l, pl.multiple_of(src[j], F32_SUBLANES), 0))],
        out_specs=pl.BlockSpec((None, PACK_BLOCK, K), lambda l, j, src, nrows: (l, j, 0)),
    )
    return pl.pallas_call(
        _pack_rows_kernel,
        grid_spec=grid_spec,
        out_shape=jax.ShapeDtypeStruct((L, n_out, K), BF16),
        compiler_params=_cparams(("parallel", "arbitrary"), 40),
        name="pack_w_in",
    )(src, nrows, w_t)


def _cast_kernel(x_ref, o_ref):
    o_ref[...] = x_ref[...].astype(o_ref.dtype)


def _cast_bf16(w):
    L, R, C = w.shape
    tr, tc = min(1024, R), min(2048, C)
    assert R % tr == 0 and C % tc == 0
    spec = pl.BlockSpec((None, tr, tc), lambda l, i, j: (l, i, j))
    return pl.pallas_call(
        _cast_kernel,
        grid=(L, R // tr, C // tc),
        in_specs=[spec],
        out_specs=spec,
        out_shape=jax.ShapeDtypeStruct((L, R, C), BF16),
        compiler_params=_cparams(("parallel", "parallel", "parallel"), 40),
        name="cast_bf16",
    )(w)


def _rope3(t, tab_ref, shift):
    return (t * tab_ref[0] + pltpu.roll(t, LANES - shift, 1) * tab_ref[1]
            + pltpu.roll(t, shift, 1) * tab_ref[2])


def _prep_kernel(q_ref, k_ref, qi_ref, kw_ref, tab_ref, qo_ref, ko_ref, qio_ref, kio_ref, wo_ref):
    scale = float(HEAD_DIM) ** -0.5
    for h in range(q_ref.shape[1] // LANES):
        sl = slice(h * LANES, (h + 1) * LANES)
        qo_ref[:, sl] = (_rope3(q_ref[:, sl].astype(F32), tab_ref.at[0], ROT_DIM // 2) * scale).astype(BF16)
    for h in range(k_ref.shape[1] // LANES):
        sl = slice(h * LANES, (h + 1) * LANES)
        ko_ref[:, sl] = _rope3(k_ref[:, sl].astype(F32), tab_ref.at[0], ROT_DIM // 2).astype(BF16)
    for h in range(qi_ref.shape[1] // LANES):
        sl = slice(h * LANES, (h + 1) * LANES)
        qio_ref[:, sl] = _rope3(qi_ref[:, sl].astype(F32), tab_ref.at[1], IDX_ROT_DIM // 2).astype(BF16)
    r = _rope3(kw_ref[...].astype(F32), tab_ref.at[2], IDX_ROT_DIM // 2)
    swapped = pltpu.roll(r, IDX_DIM, 1)
    lane = lax.broadcasted_iota(jnp.int32, r.shape, 1)
    kio_ref[...] = jnp.where(lane < IDX_DIM, r, swapped).astype(BF16)
    wo_ref[...] = swapped


def _prep(proj, tabs, lay, attn_w, kv_w, idx_w, S):
    M = proj.shape[0]
    tm = min(512, S)

    def col(name, width):
        c = lay.off[name] // width
        return pl.BlockSpec((tm, width), lambda i: (i, c))

    row = lambda width: pl.BlockSpec((tm, width), lambda i: (i, 0))
    return pl.pallas_call(
        _prep_kernel,
        grid=(M // tm,),
        in_specs=[col("q", attn_w), col("k", kv_w), col("q_idx", idx_w), col("kw", LANES),
                  pl.BlockSpec((3, 3, tm, LANES), lambda i: (0, 0, i, 0))],
        out_specs=[row(attn_w), row(kv_w), row(idx_w), row(LANES), row(LANES)],
        out_shape=[jax.ShapeDtypeStruct((M, attn_w), BF16), jax.ShapeDtypeStruct((M, kv_w), BF16),
                   jax.ShapeDtypeStruct((M, idx_w), BF16), jax.ShapeDtypeStruct((M, LANES), BF16),
                   jax.ShapeDtypeStruct((M, LANES), F32)],
        compiler_params=_cparams(("parallel",), 32),
        name="rope_prep",
    )(proj, proj, proj, proj, tabs)


LRU_HALO = 8
POOL_HALO = 16


def _gelu_tanh(x):
    return 0.5 * x * (1.0 + jnp.tanh(0.7978845608028654 * (x + 0.044715 * (x * x * x))))


def _seq_kernel(ul_ref, ug_ref, up_ref, cw_ref, cb_ref, wa_ref, ba_ref, wx_ref, bx_ref, lam_ref,
                pw_ref, ps_ref, lo_ref, po_ref, ext_l, ext_p, h_ref, a_s, b_s, *, ts):
    s = pl.program_id(1)

    @pl.when(s == 0)
    def _():
        ext_l[0:LRU_HALO, :] = jnp.zeros((LRU_HALO, ext_l.shape[1]), F32)
        ext_p[0:POOL_HALO, :] = jnp.zeros((POOL_HALO, ext_p.shape[1]), F32)
        h_ref[...] = jnp.zeros_like(h_ref)

    ext_l[LRU_HALO:LRU_HALO + ts, :] = ul_ref[...].astype(F32)
    nblk, bw = wa_ref.shape[0], wa_ref.shape[1]
    for h in range(nblk):
        sl = slice(h * bw, (h + 1) * bw)
        xc = cb_ref[:, sl]
        for j in range(CONV_WIDTH):
            xc = xc + cw_ref[j:j + 1, sl] * ext_l[pl.ds(LRU_HALO - (CONV_WIDTH - 1) + j, ts), sl]
        xb = xc.astype(BF16)
        r = jax.nn.sigmoid(_dot(xb, wa_ref[h]) + ba_ref[:, sl])
        i = jax.nn.sigmoid(_dot(xb, wx_ref[h]) + bx_ref[:, sl])
        nl = -lam_ref[:, sl]
        softplus = jnp.maximum(nl, 0.0) + jnp.log1p(jnp.exp(-jnp.abs(nl)))
        a = jnp.exp((-LRU_C) * r * softplus)
        a_s[:, sl] = a
        b_s[:, sl] = jnp.sqrt(1.0 - a * a) * (i * xc)
    ext_l[0:LRU_HALO, :] = ext_l[ts:ts + LRU_HALO, :]

    def step(t, h):
        h = a_s[pl.ds(t, 1), :] * h + b_s[pl.ds(t, 1), :]
        b_s[pl.ds(t, 1), :] = h
        return h

    h_ref[...] = lax.fori_loop(0, ts, step, h_ref[...], unroll=8)
    lo_ref[...] = (b_s[...] * _gelu_tanh(ug_ref[...].astype(F32))).astype(BF16)

    ext_p[POOL_HALO:POOL_HALO + ts, :] = up_ref[...].astype(F32)
    pg = pw_ref.shape[1]
    t_glob = s * ts + lax.broadcasted_iota(jnp.int32, (ts, 1), 0)
    for g, w in enumerate(POOL_WINDOWS):
        sl = slice(g * pg, (g + 1) * pg)
        tot = ext_p[POOL_HALO:POOL_HALO + ts, sl]
        cur = tot
        for j in range(1, w):
            tot = tot + ext_p[pl.ds(POOL_HALO - j, ts), sl]
        cnt = jnp.minimum(t_glob + 1, w).astype(F32)
        pooled = (tot / cnt - cur).astype(BF16)
        po_ref[:, sl] = (_dot(pooled, pw_ref[g]) * ps_ref[:, sl]).astype(BF16)
    ext_p[0:POOL_HALO, :] = ext_p[ts:ts + POOL_HALO, :]


def _seq(proj, lay, B, S, conv_w, conv_b, w_a, b_a, w_x, b_x, lam, pool_w, pool_scale):
    M = proj.shape[0]
    lw = conv_w.shape[-1]
    pw = pool_scale.shape[-1]
    ts = min(256, S)
    nt = S // ts

    def col(name, width):
        c = lay.off[name] // width
        return pl.BlockSpec((ts, width), lambda b, s: (b * nt + s, c))

    full = lambda a: pl.BlockSpec(a.shape, lambda b, s: (0,) * a.ndim)
    small = [conv_w, conv_b.reshape(1, lw), w_a, b_a.reshape(1, lw), w_x, b_x.reshape(1, lw),
             lam.reshape(1, lw), pool_w, pool_scale.reshape(1, pw)]
    return pl.pallas_call(
        functools.partial(_seq_kernel, ts=ts),
        grid=(B, nt),
        in_specs=[col("u_lru", lw), col("u_gate", lw), col("u_pool", pw)] + [full(a) for a in small],
        out_specs=[pl.BlockSpec((ts, lw), lambda b, s: (b * nt + s, 0)),
                   pl.BlockSpec((ts, pw), lambda b, s: (b * nt + s, 0))],
        out_shape=[jax.ShapeDtypeStruct((M, lw), BF16), jax.ShapeDtypeStruct((M, pw), BF16)],
        scratch_shapes=[pltpu.VMEM((LRU_HALO + ts, lw), F32), pltpu.VMEM((POOL_HALO + ts, pw), F32),
                        pltpu.VMEM((1, lw), F32), pltpu.VMEM((ts, lw), F32), pltpu.VMEM((ts, lw), F32)],
        compiler_params=_cparams(("arbitrary", "arbitrary"), 48),
        name="lru_pool",
    )(proj, proj, proj, *small)


COUNT_ROWS = 64


def _attn_kernel(q_ref, qi_ref, w_ref, k_ref, v_ref, ki_ref, o_ref, keys_ref, cut_ref, qs_ref, m_ref, acc_ref,
                 *, tq, tk, topk):
    qb = pl.program_id(1)
    n_chunks = ((qb + 1) * tq + tk - 1) // tk
    row_pos = qb * tq + lax.broadcasted_iota(jnp.int32, (tq, 1), 0)
    low_half = lax.broadcasted_iota(jnp.int32, (1, LANES), 1) < IDX_DIM
    high_half = lax.broadcasted_iota(jnp.int32, (1, LANES), 1) >= IDX_DIM
    n_pairs = qi_ref.shape[1] // LANES

    def score_chunk(c, carry):
        start = pl.multiple_of(c * tk, tk)
        kic = ki_ref[pl.ds(start, tk), :]
        acc = jnp.zeros((tq, tk), F32)
        for p in range(n_pairs):
            qp = qi_ref[:, p * LANES:(p + 1) * LANES]
            for half in range(2):
                qm = jnp.where(low_half if half == 0 else high_half, qp, jnp.zeros_like(qp))
                hd = 2 * p + half
                acc = acc + w_ref[:, hd:hd + 1] * jnp.maximum(_dot_nt(qm, kic), 0.0)
        bits = lax.bitcast_convert_type(acc, jnp.int32)
        bits = jnp.where(bits == INT_MIN, 0, bits)
        key = bits ^ ((bits >> 31) & 0x7FFFFFFF)
        col_pos = start + lax.broadcasted_iota(jnp.int32, (1, tk), 1)
        keys_ref[c] = jnp.where(col_pos <= row_pos, key, INT_MIN)
        return carry

    lax.fori_loop(0, n_chunks, score_chunk, 0)

    def bisect(it, thr):
        cand = thr + jnp.left_shift(jnp.int32(1), 31 - it)

        parts = []
        for rb in range(tq // COUNT_ROWS):
            rows = slice(rb * COUNT_ROWS, (rb + 1) * COUNT_ROWS)
            cand_b = jnp.broadcast_to(cand[rows], (COUNT_ROWS, LANES))

            def count_chunk(c, acc, rows=rows, cand_b=cand_b):
                for u in range(tk // LANES):
                    acc = acc + jnp.where(keys_ref[c, rows, u * LANES:(u + 1) * LANES] >= cand_b, 1.0, 0.0)
                return acc

            parts.append(lax.fori_loop(0, n_chunks, count_chunk, jnp.zeros((COUNT_ROWS, LANES), F32)))
        cnt = jnp.sum(jnp.concatenate(parts, axis=0), axis=-1, keepdims=True)
        return jnp.where(cnt >= float(topk), cand, thr)

    thr = lax.fori_loop(0, 32, bisect, jnp.full((tq, 1), INT_MIN, jnp.int32))
    thr = jnp.maximum(thr, INT_MIN + 1)

    def count_rows(make_hit):
        parts = []
        for rb in range(tq // COUNT_ROWS):
            rows = slice(rb * COUNT_ROWS, (rb + 1) * COUNT_ROWS)
            hit = make_hit(rows)

            def count_chunk(c, acc, rows=rows, hit=hit):
                for u in range(tk // LANES):
                    keyb = keys_ref[c, rows, u * LANES:(u + 1) * LANES]
                    acc = acc + jnp.where(hit(keyb, c * tk + u * LANES), 1.0, 0.0)
                return acc

            parts.append(lax.fori_loop(0, n_chunks, count_chunk, jnp.zeros((COUNT_ROWS, LANES), F32)))
        return jnp.sum(jnp.concatenate(parts, axis=0), axis=-1, keepdims=True)

    def row_block(x, rows):
        return jnp.broadcast_to(x[rows], (COUNT_ROWS, LANES))

    tied = count_rows(lambda rows: (lambda keyb, col0, t=row_block(thr, rows): keyb >= t)) > float(topk)
    cut_ref[...] = jnp.full(cut_ref.shape, INT_MAX, jnp.int32)

    @pl.when(jnp.sum(jnp.where(tied, 1.0, 0.0)) > 0.0)
    def _():
        above = count_rows(lambda rows: (lambda keyb, col0, t=row_block(thr, rows): keyb > t))
        keep = float(topk) - above
        lane_col = lax.broadcasted_iota(jnp.int32, (COUNT_ROWS, LANES), 1)
        n_bits = (k_ref.shape[0] - 1).bit_length()

        def widen(it, cut):
            cand = cut + jnp.left_shift(jnp.int32(1), n_bits - 1 - it)

            def make_hit(rows):
                t, cb = row_block(thr, rows), row_block(cand, rows)
                return lambda keyb, col0: jnp.logical_and(keyb == t, col0 + lane_col < cb)

            return jnp.where(count_rows(make_hit) < keep, cand, cut)

        cut = lax.fori_loop(0, n_bits, widen, jnp.zeros((tq, 1), jnp.int32))
        cut_ref[...] = jnp.broadcast_to(jnp.where(tied, cut, INT_MAX), cut_ref.shape)

    cut = cut_ref[:, 0:1]

    def selection_bias(c):
        key = keys_ref[c]
        col = c * tk + lax.broadcasted_iota(jnp.int32, (tq, tk), 1)
        kept_tie = jnp.logical_and(key == thr, col <= cut)
        return jnp.where(jnp.logical_or(key > thr, kept_tie), 0.0, MASK_NEG)

    n_heads = q_ref.shape[1] // HEAD_DIM
    group = n_heads // N_KV_HEADS
    gr = group * tq
    n_sub = tk // LANES
    for h in range(n_heads):
        qs_ref[h * tq:(h + 1) * tq, :] = q_ref[:, h * HEAD_DIM:(h + 1) * HEAD_DIM]

    def masked_logits(c, bias, g):
        start = pl.multiple_of(c * tk, tk)
        logits = _dot_nt(qs_ref[g * gr:(g + 1) * gr, :], k_ref[pl.ds(start, tk), g * HEAD_DIM:(g + 1) * HEAD_DIM])
        return (logits.reshape(group, tq, tk) + bias[None]).reshape(gr, tk)

    m_ref[...] = jnp.full(m_ref.shape, MASK_NEG, F32)

    def max_chunk(c, carry):
        bias = selection_bias(c)
        for g in range(N_KV_HEADS):
            logits = masked_logits(c, bias, g)
            m_part = m_ref[g * gr:(g + 1) * gr, :]
            for u in range(n_sub):
                m_part = jnp.maximum(m_part, logits[:, u * LANES:(u + 1) * LANES])
            m_ref[g * gr:(g + 1) * gr, :] = m_part
        return carry

    lax.fori_loop(0, n_chunks, max_chunk, 0)
    for g in range(N_KV_HEADS):
        rows = slice(g * gr, (g + 1) * gr)
        m_ref[rows, :] = jnp.broadcast_to(jnp.max(m_ref[rows, :], axis=-1, keepdims=True), (gr, LANES))

    acc_ref[...] = jnp.zeros(acc_ref.shape, F32)

    def acc_chunk(c, carry):
        start = pl.multiple_of(c * tk, tk)
        bias = selection_bias(c)
        ones = jnp.ones((tk, HEAD_DIM), BF16)
        for g in range(N_KV_HEADS):
            rows = slice(g * gr, (g + 1) * gr)
            logits = masked_logits(c, bias, g)
            m_row = m_ref[rows, :]
            p = jnp.concatenate([jnp.exp(logits[:, u * LANES:(u + 1) * LANES] - m_row) for u in range(n_sub)],
                                axis=1).astype(BF16)
            v_ext = jnp.concatenate([v_ref[pl.ds(start, tk), g * HEAD_DIM:(g + 1) * HEAD_DIM], ones], axis=1)
            acc_ref[rows, :] += _dot(p, v_ext)
        return carry

    lax.fori_loop(0, n_chunks, acc_chunk, 0)
    for h in range(n_heads):
        rows = slice(h * tq, (h + 1) * tq)
        o_ref[:, h * HEAD_DIM:(h + 1) * HEAD_DIM] = (
            acc_ref[rows, :HEAD_DIM] / acc_ref[rows, HEAD_DIM:HEAD_DIM + 1]).astype(BF16)


def _attention(q, qi, w, k, proj, ki, lay, B, S):
    M, attn_w = q.shape
    kv_w = k.shape[1]
    idx_w = qi.shape[1]
    tq = min(512, S)
    tk = min(512, S)
    nq = S // tq
    topk = min(TOPK_MAX, S // 4)
    v_col = lay.off["v"] // kv_w
    return pl.pallas_call(
        functools.partial(_attn_kernel, tq=tq, tk=tk, topk=topk),
        grid=(B, nq),
        in_specs=[
            pl.BlockSpec((tq, attn_w), lambda b, i: (b * nq + i, 0)),
            pl.BlockSpec((tq, idx_w), lambda b, i: (b * nq + i, 0)),
            pl.BlockSpec((tq, LANES), lambda b, i: (b * nq + i, 0)),
            pl.BlockSpec((S, kv_w), lambda b, i: (b, 0), pipeline_mode=pl.Buffered(1)),
            pl.BlockSpec((S, kv_w), lambda b, i: (b, v_col), pipeline_mode=pl.Buffered(1)),
            pl.BlockSpec((S, LANES), lambda b, i: (b, 0), pipeline_mode=pl.Buffered(1)),
        ],
        out_specs=pl.BlockSpec((tq, attn_w), lambda b, i: (b * nq + i, 0)),
        out_shape=jax.ShapeDtypeStruct((M, attn_w), BF16),
        scratch_shapes=[pltpu.VMEM((S // tk, tq, tk), jnp.int32),
                        pltpu.VMEM((tq, LANES), jnp.int32),
                        pltpu.VMEM((attn_w // HEAD_DIM * tq, HEAD_DIM), BF16),
                        pltpu.VMEM((attn_w // HEAD_DIM * tq, LANES), F32),
                        pltpu.VMEM((attn_w // HEAD_DIM * tq, 2 * HEAD_DIM), F32)],
        compiler_params=_cparams(("parallel", "arbitrary"), 56),
        name="indexer_attention",
    )(q, qi, w, k, proj, ki)


def _branch_kernel(al_ref, aa_ref, ap_ref, g0_ref, g1_ref, g2_ref, wl_ref, wa_ref, wp_ref, o_ref):
    merged = (jax.nn.sigmoid(g0_ref[...].astype(F32)) * _dot(al_ref[...], wl_ref[...])
              + jax.nn.sigmoid(g1_ref[...].astype(F32)) * _dot(aa_ref[...], wa_ref[...])
              + jax.nn.sigmoid(g2_ref[...].astype(F32)) * _dot(ap_ref[...], wp_ref[...]))
    o_ref[...] = merged.astype(o_ref.dtype)


def _branch_merge(l, a_lru, a_attn, a_pool, proj, lay, w_l, w_a, w_p, S):
    M = a_lru.shape[0]
    D = w_l.shape[-1]
    tm = min(1024, S)
    tn = min(512, D)
    g_base = lay.off["g_br"] // tn
    nj = D // tn

    def gspec(br):
        return pl.BlockSpec((tm, tn), lambda i, j: (i, g_base + br * nj + j))

    def aspec(a):
        return pl.BlockSpec((tm, a.shape[1]), lambda i, j: (i, 0), pipeline_mode=pl.Buffered(1))

    def wspec(w):
        return pl.BlockSpec((None, w.shape[1], tn), lambda i, j: (l, 0, j))

    return pl.pallas_call(
        _branch_kernel,
        grid=(M // tm, nj),
        in_specs=[aspec(a_lru), aspec(a_attn), aspec(a_pool), gspec(0), gspec(1), gspec(2),
                  wspec(w_l), wspec(w_a), wspec(w_p)],
        out_specs=pl.BlockSpec((tm, tn), lambda i, j: (i, j)),
        out_shape=jax.ShapeDtypeStruct((M, D), BF16),
        compiler_params=_cparams(("parallel", "arbitrary"), 56),
        name="branch_merge",
    )(a_lru, a_attn, a_pool, proj, proj, proj, w_l, w_a, w_p)


def _outproj_kernel(x_ref, m_ref, gpost_ref, a_ref, w_ref, o_ref):
    j = pl.program_id(1)

    @pl.when(j == 0)
    def _():
        o_ref[...] = jnp.zeros_like(o_ref)

    o_ref[...] += _dot(a_ref[...], w_ref[...])

    @pl.when(j == pl.num_programs(1) - 1)
    def _():
        _postnorm_residual(x_ref, o_ref, m_ref, gpost_ref, 1, 1.0)


def _outproj(x, mods, l, gpost, merged, w_o, S):
    M, D = x.shape
    K = merged.shape[1]
    tm = min(512, S)
    tk = min(1024, K)
    tpb = S // tm
    return pl.pallas_call(
        _outproj_kernel,
        grid=(M // tm, K // tk),
        in_specs=[
            pl.BlockSpec((tm, D), lambda i, j: (i, 0), pipeline_mode=pl.Buffered(1)),
            pl.BlockSpec((None, None, N_MOD, D), lambda i, j: (l, i // tpb, 0, 0)),
            pl.BlockSpec((1, D), lambda i, j: (0, 0)),
            pl.BlockSpec((tm, tk), lambda i, j: (i, j)),
            pl.BlockSpec((None, tk, D), lambda i, j: (l, j, 0)),
        ],
        out_specs=pl.BlockSpec((tm, D), lambda i, j: (i, 0)),
        out_shape=jax.ShapeDtypeStruct((M, D), F32),
        compiler_params=_cparams(("parallel", "arbitrary"), 56),
        name="out_proj",
    )(x, mods, gpost.reshape(1, D), merged, w_o)


def kernel(x, c, positions, w_mod, b_mod, mod_offset, norm_pre, norm_post, ffn1_w_gate, ffn1_w_up, ffn1_w_down, w_in, conv_w, conv_b, lru_w_a, lru_b_a, lru_w_x, lru_b_x, lru_lambda, pool_w, pool_scale, w_br_lru, w_br_attn, w_br_pool, w_out, ffn2_w_gate, ffn2_w_up, ffn2_w_down):
    B, S, D = x.shape
    M = B * S
    depth = w_in.shape[0]
    lru_w = conv_w.shape[-1]
    attn_w = w_br_attn.shape[1]
    kv_w = N_KV_HEADS * HEAD_DIM
    idx_w = IDX_HEADS * IDX_DIM
    pool_wd = pool_scale.shape[-1]
    tn_proj = 768
    lay = _Layout(D, lru_w, attn_w, kv_w, idx_w, pool_wd, tn_proj)
    assert lay.d_in == w_in.shape[-1]
    assert B <= MOD_ROWS and S % LANES == 0

    bf = _cast_bf16
    w_in_p = lay.pack_t(w_in)
    f1g, f1u, f1d = bf(ffn1_w_gate), bf(ffn1_w_up), bf(ffn1_w_down)
    f2g, f2u, f2d = bf(ffn2_w_gate), bf(ffn2_w_up), bf(ffn2_w_down)
    wbl, wba, wbp, wo = bf(w_br_lru), bf(w_br_attn), bf(w_br_pool), bf(w_out)
    wa, wx, pw = lru_w_a.astype(BF16), lru_w_x.astype(BF16), pool_w.astype(BF16)

    mods = _modulation(c, w_mod, b_mod, mod_offset)
    tabs = _rope_tables(positions)

    xf = x.reshape(M, D)
    for l in range(depth):
        xf = _ffn(xf, mods, l, 0, norm_pre[l, 0], norm_post[l, 0], f1g, f1u, f1d, S)
        proj = _proj(xf, mods, l, norm_pre[l, 1], w_in_p, S, tn_proj)
        q, k, qi, ki, w = _prep(proj, tabs, lay, attn_w, kv_w, idx_w, S)
        a_lru, a_pool = _seq(proj, lay, B, S, conv_w[l], conv_b[l], wa[l], lru_b_a[l], wx[l], lru_b_x[l],
                             lru_lambda[l], pw[l], pool_scale[l])
        a_attn = _attention(q, qi, w, k, proj, ki, lay, B, S)
        merged = _branch_merge(l, a_lru, a_attn, a_pool, proj, lay, wbl, wba, wbp, S)
        xf = _outproj(xf, mods, l, norm_post[l, 1], merged, wo, S)
        xf = _ffn(xf, mods, l, 2, norm_pre[l, 2], norm_post[l, 2], f2g, f2u, f2d, S)
    return xf.reshape(B, S, D)
```

```python
import functools

import numpy as np
import jax
import jax.numpy as jnp
from jax import lax
from jax.experimental import pallas as pl
from jax.experimental.pallas import tpu as pltpu

F32 = jnp.float32
BF16 = jnp.bfloat16

N_MOD = 9
FFN_RES = 0.5
RMS_EPS = 1e-6
CONV_WIDTH = 4
LRU_C = 8.0
HEAD_DIM = 128
N_KV_HEADS = 4
ROT_DIM = HEAD_DIM // 4
IDX_HEADS = 16
IDX_DIM = 64
IDX_ROT_DIM = IDX_DIM // 4
TOPK_MAX = 256
ROPE_THETA = 500000.0
POOL_WINDOWS = (2, 4, 8, 16)

LANES = 128
MOD_ROWS = 8
MIB = 1024 * 1024
INT_MIN = -2 ** 31
INT_MAX = 2 ** 31 - 1
MASK_NEG = -1e30


def _cparams(semantics, vmem_mib):
    return pltpu.CompilerParams(dimension_semantics=semantics, vmem_limit_bytes=vmem_mib * MIB)


def _dot(a, b):
    return jnp.dot(a, b, preferred_element_type=F32)


def _dot_nt(a, b):
    return lax.dot_general(a, b, (((1,), (1,)), ((), ())), preferred_element_type=F32)


def _mod_kernel(c_ref, w_ref, b_ref, off_ref, o_ref):
    c = c_ref[...]
    s = (c * jax.nn.sigmoid(c)).astype(BF16)
    base = _dot(s, w_ref[...].astype(BF16)) + b_ref[...]
    for l in range(o_ref.shape[0]):
        o_ref[l] = base + off_ref[l:l + 1, :]


def _modulation(c, w_mod, b_mod, mod_offset):
    B, D = c.shape
    depth = mod_offset.shape[0]
    n = w_mod.shape[1]
    tn = min(1024, n)
    c_pad = jnp.zeros((MOD_ROWS, D), F32).at[:B].set(c)
    out = pl.pallas_call(
        _mod_kernel,
        grid=(n // tn,),
        in_specs=[
            pl.BlockSpec((MOD_ROWS, D), lambda j: (0, 0)),
            pl.BlockSpec((D, tn), lambda j: (0, j)),
            pl.BlockSpec((1, tn), lambda j: (0, j)),
            pl.BlockSpec((depth, tn), lambda j: (0, j)),
        ],
        out_specs=pl.BlockSpec((depth, MOD_ROWS, tn), lambda j: (0, 0, j)),
        out_shape=jax.ShapeDtypeStruct((depth, MOD_ROWS, n), F32),
        compiler_params=_cparams(("arbitrary",), 48),
        name="modulation",
    )(c_pad, w_mod, b_mod.reshape(1, n), mod_offset.reshape(depth, n))
    return out.reshape(depth, MOD_ROWS, N_MOD, D)


def _tables_kernel(pos_ref, pat_ref, o_ref):
    pos = pos_ref[...]
    for t in range(3):
        ang = pos * pat_ref[t, 0:1, :]
        cos = jnp.cos(ang)
        sin = jnp.sin(ang)
        o_ref[t, 0] = cos * pat_ref[t, 1:2, :] + pat_ref[t, 2:3, :]
        o_ref[t, 1] = sin * pat_ref[t, 3:4, :]
        o_ref[t, 2] = sin * pat_ref[t, 4:5, :]


def _rope_pattern(head_dim, rot_dim, n_lanes):
    half = rot_dim // 2
    inv_freq = ROPE_THETA ** (-jnp.arange(half, dtype=F32) / half)
    d = np.arange(n_lanes) % head_dim
    in_rot = d < rot_dim
    invf = jnp.where(in_rot, inv_freq[d % half], 0.0)
    rows = [invf,
            jnp.asarray(in_rot, F32),
            jnp.asarray(~in_rot, F32),
            jnp.asarray(-(d < half).astype(np.float32)),
            jnp.asarray(((d >= half) & in_rot).astype(np.float32))]
    return jnp.stack(rows + [jnp.zeros((n_lanes,), F32)] * 3)


def _rope_tables(positions):
    M = positions.size
    attn = _rope_pattern(HEAD_DIM, ROT_DIM, LANES)
    idx = _rope_pattern(IDX_DIM, IDX_ROT_DIM, LANES)
    lane = np.arange(LANES)
    is_k = jnp.asarray(lane < IDX_DIM, F32)
    is_w = (lane >= IDX_DIM) & (lane < IDX_DIM + IDX_HEADS)
    w_scale = float(IDX_HEADS * IDX_DIM) ** -0.5
    kw = idx * is_k[None, :]
    kw = kw.at[2].set(idx[2] * is_k + jnp.asarray(is_w, F32) * w_scale)
    pat = jnp.stack([attn, idx, kw])
    tm = min(512, M)
    return pl.pallas_call(
        _tables_kernel,
        grid=(M // tm,),
        in_specs=[pl.BlockSpec((tm, 1), lambda i: (i, 0)),
                  pl.BlockSpec((3, 8, LANES), lambda i: (0, 0, 0))],
        out_specs=pl.BlockSpec((3, 3, tm, LANES), lambda i: (0, 0, i, 0)),
        out_shape=jax.ShapeDtypeStruct((3, 3, M, LANES), F32),
        compiler_params=_cparams(("parallel",), 32),
        name="rope_tables",
    )(positions.reshape(M, 1).astype(F32), pat)


NORM_ROWS = 64


def _prenorm_to(x_ref, dst_ref, m_ref, gpre_ref, sub):
    mul = gpre_ref[...] * (1.0 + m_ref[3 * sub + 1:3 * sub + 2, :])
    shift = m_ref[3 * sub:3 * sub + 1, :]

    def body(r, carry):
        rows = pl.ds(pl.multiple_of(r * NORM_ROWS, NORM_ROWS), NORM_ROWS)
        x = x_ref[rows, :]
        rs = lax.rsqrt(jnp.mean(x * x, axis=-1, keepdims=True) + RMS_EPS)
        dst_ref[rows, :] = (x_ref[rows, :] * rs * mul + shift).astype(BF16)
        return carry

    lax.fori_loop(0, x_ref.shape[0] // NORM_ROWS, body, 0)


def _postnorm_residual(x_ref, o_ref, m_ref, gpost_ref, sub, res):
    mul = (res * m_ref[3 * sub + 2:3 * sub + 3, :]) * gpost_ref[...]

    def body(r, carry):
        rows = pl.ds(pl.multiple_of(r * NORM_ROWS, NORM_ROWS), NORM_ROWS)
        y = o_ref[rows, :]
        rs = lax.rsqrt(jnp.mean(y * y, axis=-1, keepdims=True) + RMS_EPS)
        o_ref[rows, :] = x_ref[rows, :] + o_ref[rows, :] * rs * mul
        return carry

    lax.fori_loop(0, x_ref.shape[0] // NORM_ROWS, body, 0)


FFN_TF = 512


def _ffn_kernel(x_ref, m_ref, gpre_ref, gpost_ref, wgu_ref, wd_ref, o_ref, xn_ref, *, sub):
    j = pl.program_id(1)

    @pl.when(j == 0)
    def _():
        _prenorm_to(x_ref, xn_ref, m_ref, gpre_ref, sub)
        o_ref[...] = jnp.zeros_like(o_ref)

    tf = wd_ref.shape[0]
    hu = _dot(xn_ref[...], wgu_ref[...])
    h, u = hu[:, :tf], hu[:, tf:]
    o_ref[...] += _dot((h * jax.nn.sigmoid(h) * u).astype(BF16), wd_ref[...])

    @pl.when(j == pl.num_programs(1) - 1)
    def _():
        _postnorm_residual(x_ref, o_ref, m_ref, gpost_ref, sub, FFN_RES)


def _ffn(x, mods, l, sub, gpre, gpost, wgu, wd, S):
    M, D = x.shape
    F = wd.shape[1]
    tm = min(512, S)
    tf = min(FFN_TF, F)
    tpb = S // tm
    return pl.pallas_call(
        functools.partial(_ffn_kernel, sub=sub),
        grid=(M // tm, F // tf),
        in_specs=[
            pl.BlockSpec((tm, D), lambda i, j: (i, 0), pipeline_mode=pl.Buffered(1)),
            pl.BlockSpec((None, None, N_MOD, D), lambda i, j: (l, i // tpb, 0, 0)),
            pl.BlockSpec((1, D), lambda i, j: (0, 0)),
            pl.BlockSpec((1, D), lambda i, j: (0, 0)),
            pl.BlockSpec((None, D, 2 * tf), lambda i, j: (l, 0, j)),
            pl.BlockSpec((None, tf, D), lambda i, j: (l, j, 0)),
        ],
        out_specs=pl.BlockSpec((tm, D), lambda i, j: (i, 0)),
        out_shape=jax.ShapeDtypeStruct((M, D), F32),
        scratch_shapes=[pltpu.VMEM((tm, D), BF16)],
        compiler_params=_cparams(("parallel", "arbitrary"), 60),
        name=f"ffn{sub}",
    )(x, mods, gpre.reshape(1, D), gpost.reshape(1, D), wgu, wd)


def _proj_kernel(x_ref, m_ref, gpre_ref, w_ref, o_ref, xn_ref):
    @pl.when(pl.program_id(1) == 0)
    def _():
        _prenorm_to(x_ref, xn_ref, m_ref, gpre_ref, 1)

    o_ref[...] = _dot_nt(xn_ref[...], w_ref[...]).astype(o_ref.dtype)


def _proj(x, mods, l, gpre, w, S, tn):
    M, D = x.shape
    N = w.shape[1]
    tm = min(1024, S)
    tpb = S // tm
    return pl.pallas_call(
        _proj_kernel,
        grid=(M // tm, N // tn),
        in_specs=[
            pl.BlockSpec((tm, D), lambda i, j: (i, 0), pipeline_mode=pl.Buffered(1)),
            pl.BlockSpec((None, None, N_MOD, D), lambda i, j: (l, i // tpb, 0, 0)),
            pl.BlockSpec((1, D), lambda i, j: (0, 0)),
            pl.BlockSpec((None, tn, D), lambda i, j: (l, j, 0)),
        ],
        out_specs=pl.BlockSpec((tm, tn), lambda i, j: (i, j)),
        out_shape=jax.ShapeDtypeStruct((M, N), BF16),
        scratch_shapes=[pltpu.VMEM((tm, D), BF16)],
        compiler_params=_cparams(("parallel", "arbitrary"), 60),
        name="mixer_proj",
    )(x, mods, gpre.reshape(1, D), w)


class _Layout:
    def __init__(self, D, lru_w, attn_w, kv_w, idx_w, pool_w, tn):
        split = (lru_w, lru_w, attn_w, kv_w, kv_w, idx_w, IDX_DIM, IDX_HEADS, pool_w, 3 * D)
        names = ("u_lru", "u_gate", "q", "k", "v", "q_idx", "k_idx", "w_idx", "u_pool", "g_br")
        src = dict(zip(names, zip(np.cumsum((0,) + split[:-1]).tolist(), split)))
        segs = [("u_lru", lru_w, ["u_lru"]), ("u_gate", lru_w, ["u_gate"]), ("q", attn_w, ["q"]),
                ("u_pool", pool_w, ["u_pool"]), ("g_br", D, ["g_br"]), ("k", kv_w, ["k"]),
                ("v", kv_w, ["v"]), ("q_idx", idx_w, ["q_idx"]), ("kw", LANES, ["k_idx", "w_idx"])]

        def place(order):
            off, pieces, pos = {}, [], 0
            for name, width, parts in order:
                if pos % width:
                    return None
                off[name] = pos
                length = 0
                for p in parts:
                    pieces.append(src[p])
                    length += src[p][1]
                padded = -(-length // width) * width
                if padded != length:
                    pieces.append((None, padded - length))
                pos += padded
            return off, pieces, pos

        self.off, pieces, pos = place(segs) or place(sorted(segs, key=lambda s: -s[1]))
        self.n = -(-pos // tn) * tn
        if self.n != pos:
            pieces.append((None, self.n - pos))
        self.pieces = []
        for start, length in pieces:
            last = self.pieces[-1] if self.pieces else None
            if last and (start is None) == (last[0] is None) and (start is None or last[0] + last[1] == start):
                self.pieces[-1] = (last[0], last[1] + length)
            else:
                self.pieces.append((start, length))
        self.d_in = sum(split)

    def pack_plan(self):
        spans, dest = [], 0
        for s, n in self.pieces:
            spans.append((dest, n, s))
            dest += n
        if self.n % PACK_BLOCK:
            return None
        plan = []
        for d0 in range(0, self.n, PACK_BLOCK):
            inside = [sp for sp in spans if sp[0] < d0 + PACK_BLOCK and sp[0] + sp[1] > d0]
            first, rest = inside[0], inside[1:]
            if any(sp[2] is not None for sp in rest):
                return None
            if first[2] is None:
                plan.append((0, 0))
                continue
            c = first[2] + d0 - first[0]
            if c % F32_SUBLANES or c + PACK_BLOCK > self.d_in:
                return None
            plan.append((c, min(PACK_BLOCK, first[0] + first[1] - d0)))
        return plan

    def pack_t(self, w_in):
        w_t = jnp.swapaxes(w_in, 1, 2)
        plan = self.pack_plan()
        if plan is not None:
            return _pack_rows(w_t, plan, self.n)
        rows = [jnp.zeros(w_t.shape[:1] + (n,) + w_t.shape[2:], BF16) if s is None else w_t[:, s:s + n].astype(BF16)
                for s, n in self.pieces]
        return jnp.concatenate(rows, axis=1)


PACK_BLOCK = 512
F32_SUBLANES = 8


def _pack_rows_kernel(src, nrows, w_ref, o_ref):
    keep = lax.broadcasted_iota(jnp.int32, (PACK_BLOCK, 1), 0) < nrows[pl.program_id(1)]
    o_ref[...] = jnp.where(keep, w_ref[0], 0.0).astype(BF16)


def _pack_rows(w_t, plan, n_out):
    L, _, K = w_t.shape
    src = jnp.asarray([p[0] for p in plan], jnp.int32)
    nrows = jnp.asarray([p[1] for p in plan], jnp.int32)
    grid_spec = pltpu.PrefetchScalarGridSpec(
        num_scalar_prefetch=2,
        grid=(L, n_out // PACK_BLOCK),
        in_specs=[pl.BlockSpec((pl.Element(1), pl.Element(PACK_BLOCK), pl.Element(K)),
                               lambda l, j, src, nrows: (l, pl.multiple_of(src[j], F32_SUBLANES), 0))],
        out_specs=pl.BlockSpec((None, PACK_BLOCK, K), lambda l, j, src, nrows: (l, j, 0)),
    )
    return pl.pallas_call(
        _pack_rows_kernel,
        grid_spec=grid_spec,
        out_shape=jax.ShapeDtypeStruct((L, n_out, K), BF16),
        compiler_params=_cparams(("parallel", "arbitrary"), 40),
        name="pack_w_in",
    )(src, nrows, w_t)


def _cast_kernel(x_ref, o_ref):
    o_ref[...] = x_ref[...].astype(o_ref.dtype)


def _cast_pair_kernel(a_ref, b_ref, o_ref):
    tc = a_ref.shape[1]
    o_ref[:, :tc] = a_ref[...].astype(o_ref.dtype)
    o_ref[:, tc:] = b_ref[...].astype(o_ref.dtype)


def _cast_pair_bf16(a, b, tc):
    L, R, C = a.shape
    tr = min(1024, R)
    assert a.shape == b.shape and R % tr == 0 and C % tc == 0
    spec = pl.BlockSpec((None, tr, tc), lambda l, i, j: (l, i, j))
    return pl.pallas_call(
        _cast_pair_kernel,
        grid=(L, R // tr, C // tc),
        in_specs=[spec, spec],
        out_specs=pl.BlockSpec((None, tr, 2 * tc), lambda l, i, j: (l, i, j)),
        out_shape=jax.ShapeDtypeStruct((L, R, 2 * C), BF16),
        compiler_params=_cparams(("parallel", "parallel", "parallel"), 40),
        name="cast_pair_bf16",
    )(a, b)


def _cast_bf16(w):
    L, R, C = w.shape
    tr, tc = min(1024, R), min(2048, C)
    assert R % tr == 0 and C % tc == 0
    spec = pl.BlockSpec((None, tr, tc), lambda l, i, j: (l, i, j))
    return pl.pallas_call(
        _cast_kernel,
        grid=(L, R // tr, C // tc),
        in_specs=[spec],
        out_specs=spec,
        out_shape=jax.ShapeDtypeStruct((L, R, C), BF16),
        compiler_params=_cparams(("parallel", "parallel", "parallel"), 40),
        name="cast_bf16",
    )(w)


def _rope3(t, tab_ref, shift):
    return (t * tab_ref[0] + pltpu.roll(t, LANES - shift, 1) * tab_ref[1]
            + pltpu.roll(t, shift, 1) * tab_ref[2])


def _prep_kernel(q_ref, k_ref, qi_ref, kw_ref, tab_ref, qo_ref, ko_ref, qio_ref, kio_ref, wo_ref):
    scale = float(HEAD_DIM) ** -0.5
    for h in range(q_ref.shape[1] // LANES):
        sl = slice(h * LANES, (h + 1) * LANES)
        qo_ref[:, sl] = (_rope3(q_ref[:, sl].astype(F32), tab_ref.at[0], ROT_DIM // 2) * scale).astype(BF16)
    for h in range(k_ref.shape[1] // LANES):
        sl = slice(h * LANES, (h + 1) * LANES)
        ko_ref[:, sl] = _rope3(k_ref[:, sl].astype(F32), tab_ref.at[0], ROT_DIM // 2).astype(BF16)
    for h in range(qi_ref.shape[1] // LANES):
        sl = slice(h * LANES, (h + 1) * LANES)
        qio_ref[:, sl] = _rope3(qi_ref[:, sl].astype(F32), tab_ref.at[1], IDX_ROT_DIM // 2).astype(BF16)
    r = _rope3(kw_ref[...].astype(F32), tab_ref.at[2], IDX_ROT_DIM // 2)
    swapped = pltpu.roll(r, IDX_DIM, 1)
    lane = lax.broadcasted_iota(jnp.int32, r.shape, 1)
    kio_ref[...] = jnp.where(lane < IDX_DIM, r, swapped).astype(BF16)
    wo_ref[...] = swapped


def _prep(proj, tabs, lay, attn_w, kv_w, idx_w, S):
    M = proj.shape[0]
    tm = min(512, S)

    def col(name, width):
        c = lay.off[name] // width
        return pl.BlockSpec((tm, width), lambda i: (i, c))

    row = lambda width: pl.BlockSpec((tm, width), lambda i: (i, 0))
    return pl.pallas_call(
        _prep_kernel,
        grid=(M // tm,),
        in_specs=[col("q", attn_w), col("k", kv_w), col("q_idx", idx_w), col("kw", LANES),
                  pl.BlockSpec((3, 3, tm, LANES), lambda i: (0, 0, i, 0))],
        out_specs=[row(attn_w), row(kv_w), row(idx_w), row(LANES), row(LANES)],
        out_shape=[jax.ShapeDtypeStruct((M, attn_w), BF16), jax.ShapeDtypeStruct((M, kv_w), BF16),
                   jax.ShapeDtypeStruct((M, idx_w), BF16), jax.ShapeDtypeStruct((M, LANES), BF16),
                   jax.ShapeDtypeStruct((M, LANES), F32)],
        compiler_params=_cparams(("parallel",), 32),
        name="rope_prep",
    )(proj, proj, proj, proj, tabs)


LRU_HALO = 8
POOL_HALO = 16


def _gelu_tanh(x):
    return 0.5 * x * (1.0 + jnp.tanh(0.7978845608028654 * (x + 0.044715 * (x * x * x))))


def _seq_kernel(ul_ref, ug_ref, up_ref, cw_ref, cb_ref, wa_ref, ba_ref, wx_ref, bx_ref, lam_ref,
                pw_ref, ps_ref, lo_ref, po_ref, ext_l, ext_p, h_ref, a_s, b_s, *, ts):
    s = pl.program_id(1)

    @pl.when(s == 0)
    def _():
        ext_l[0:LRU_HALO, :] = jnp.zeros((LRU_HALO, ext_l.shape[1]), F32)
        ext_p[0:POOL_HALO, :] = jnp.zeros((POOL_HALO, ext_p.shape[1]), F32)
        h_ref[...] = jnp.zeros_like(h_ref)

    ext_l[LRU_HALO:LRU_HALO + ts, :] = ul_ref[...].astype(F32)
    nblk, bw = wa_ref.shape[0], wa_ref.shape[1]
    for h in range(nblk):
        sl = slice(h * bw, (h + 1) * bw)
        xc = cb_ref[:, sl]
        for j in range(CONV_WIDTH):
            xc = xc + cw_ref[j:j + 1, sl] * ext_l[pl.ds(LRU_HALO - (CONV_WIDTH - 1) + j, ts), sl]
        xb = xc.astype(BF16)
        r = jax.nn.sigmoid(_dot(xb, wa_ref[h]) + ba_ref[:, sl])
        i = jax.nn.sigmoid(_dot(xb, wx_ref[h]) + bx_ref[:, sl])
        nl = -lam_ref[:, sl]
        softplus = jnp.maximum(nl, 0.0) + jnp.log1p(jnp.exp(-jnp.abs(nl)))
        a = jnp.exp((-LRU_C) * r * softplus)
        a_s[:, sl] = a
        b_s[:, sl] = jnp.sqrt(1.0 - a * a) * (i * xc)
    ext_l[0:LRU_HALO, :] = ext_l[ts:ts + LRU_HALO, :]

    def step(t, h):
        h = a_s[pl.ds(t, 1), :] * h + b_s[pl.ds(t, 1), :]
        b_s[pl.ds(t, 1), :] = h
        return h

    h_ref[...] = lax.fori_loop(0, ts, step, h_ref[...], unroll=8)
    lo_ref[...] = (b_s[...] * _gelu_tanh(ug_ref[...].astype(F32))).astype(BF16)

    ext_p[POOL_HALO:POOL_HALO + ts, :] = up_ref[...].astype(F32)
    pg = pw_ref.shape[1]
    t_glob = s * ts + lax.broadcasted_iota(jnp.int32, (ts, 1), 0)
    for g, w in enumerate(POOL_WINDOWS):
        sl = slice(g * pg, (g + 1) * pg)
        tot = ext_p[POOL_HALO:POOL_HALO + ts, sl]
        cur = tot
        for j in range(1, w):
            tot = tot + ext_p[pl.ds(POOL_HALO - j, ts), sl]
        cnt = jnp.minimum(t_glob + 1, w).astype(F32)
        pooled = (tot / cnt - cur).astype(BF16)
        po_ref[:, sl] = (_dot(pooled, pw_ref[g]) * ps_ref[:, sl]).astype(BF16)
    ext_p[0:POOL_HALO, :] = ext_p[ts:ts + POOL_HALO, :]


def _seq(proj, lay, B, S, conv_w, conv_b, w_a, b_a, w_x, b_x, lam, pool_w, pool_scale):
    M = proj.shape[0]
    lw = conv_w.shape[-1]
    pw = pool_scale.shape[-1]
    ts = min(256, S)
    nt = S // ts

    def col(name, width):
        c = lay.off[name] // width
        return pl.BlockSpec((ts, width), lambda b, s: (b * nt + s, c))

    full = lambda a: pl.BlockSpec(a.shape, lambda b, s: (0,) * a.ndim)
    small = [conv_w, conv_b.reshape(1, lw), w_a, b_a.reshape(1, lw), w_x, b_x.reshape(1, lw),
             lam.reshape(1, lw), pool_w, pool_scale.reshape(1, pw)]
    return pl.pallas_call(
        functools.partial(_seq_kernel, ts=ts),
        grid=(B, nt),
        in_specs=[col("u_lru", lw), col("u_gate", lw), col("u_pool", pw)] + [full(a) for a in small],
        out_specs=[pl.BlockSpec((ts, lw), lambda b, s: (b * nt + s, 0)),
                   pl.BlockSpec((ts, pw), lambda b, s: (b * nt + s, 0))],
        out_shape=[jax.ShapeDtypeStruct((M, lw), BF16), jax.ShapeDtypeStruct((M, pw), BF16)],
        scratch_shapes=[pltpu.VMEM((LRU_HALO + ts, lw), F32), pltpu.VMEM((POOL_HALO + ts, pw), F32),
                        pltpu.VMEM((1, lw), F32), pltpu.VMEM((ts, lw), F32), pltpu.VMEM((ts, lw), F32)],
        compiler_params=_cparams(("arbitrary", "arbitrary"), 48),
        name="lru_pool",
    )(proj, proj, proj, *small)


COUNT_ROWS = 64


def _attn_kernel(q_ref, qi_ref, w_ref, k_ref, v_ref, ki_ref, o_ref, keys_ref, cut_ref, qs_ref, m_ref, acc_ref,
                 *, tq, tk, topk):
    qb = pl.program_id(1)
    n_chunks = ((qb + 1) * tq + tk - 1) // tk
    row_pos = qb * tq + lax.broadcasted_iota(jnp.int32, (tq, 1), 0)
    low_half = lax.broadcasted_iota(jnp.int32, (1, LANES), 1) < IDX_DIM
    high_half = lax.broadcasted_iota(jnp.int32, (1, LANES), 1) >= IDX_DIM
    n_pairs = qi_ref.shape[1] // LANES

    def score_chunk(c, carry):
        start = pl.multiple_of(c * tk, tk)
        kic = ki_ref[pl.ds(start, tk), :]
        acc = jnp.zeros((tq, tk), F32)
        for p in range(n_pairs):
            qp = qi_ref[:, p * LANES:(p + 1) * LANES]
            for half in range(2):
                qm = jnp.where(low_half if half == 0 else high_half, qp, jnp.zeros_like(qp))
                hd = 2 * p + half
                acc = acc + w_ref[:, hd:hd + 1] * jnp.maximum(_dot_nt(qm, kic), 0.0)
        bits = lax.bitcast_convert_type(acc, jnp.int32)
        bits = jnp.where(bits == INT_MIN, 0, bits)
        key = bits ^ ((bits >> 31) & 0x7FFFFFFF)
        col_pos = start + lax.broadcasted_iota(jnp.int32, (1, tk), 1)
        keys_ref[c] = jnp.where(col_pos <= row_pos, key, INT_MIN)
        return carry

    lax.fori_loop(0, n_chunks, score_chunk, 0)

    def bisect(it, thr):
        cand = thr + jnp.left_shift(jnp.int32(1), 31 - it)

        parts = []
        for rb in range(tq // COUNT_ROWS):
            rows = slice(rb * COUNT_ROWS, (rb + 1) * COUNT_ROWS)
            cand_b = jnp.broadcast_to(cand[rows], (COUNT_ROWS, LANES))

            def count_chunk(c, acc, rows=rows, cand_b=cand_b):
                for u in range(tk // LANES):
                    acc = acc + jnp.where(keys_ref[c, rows, u * LANES:(u + 1) * LANES] >= cand_b, 1.0, 0.0)
                return acc

            parts.append(lax.fori_loop(0, n_chunks, count_chunk, jnp.zeros((COUNT_ROWS, LANES), F32)))
        cnt = jnp.sum(jnp.concatenate(parts, axis=0), axis=-1, keepdims=True)
        return jnp.where(cnt >= float(topk), cand, thr)

    thr = lax.fori_loop(0, 32, bisect, jnp.full((tq, 1), INT_MIN, jnp.int32))
    thr = jnp.maximum(thr, INT_MIN + 1)

    def count_rows(make_hit):
        parts = []
        for rb in range(tq // COUNT_ROWS):
            rows = slice(rb * COUNT_ROWS, (rb + 1) * COUNT_ROWS)
            hit = make_hit(rows)

            def count_chunk(c, acc, rows=rows, hit=hit):
                for u in range(tk // LANES):
                    keyb = keys_ref[c, rows, u * LANES:(u + 1) * LANES]
                    acc = acc + jnp.where(hit(keyb, c * tk + u * LANES), 1.0, 0.0)
                return acc

            parts.append(lax.fori_loop(0, n_chunks, count_chunk, jnp.zeros((COUNT_ROWS, LANES), F32)))
        return jnp.sum(jnp.concatenate(parts, axis=0), axis=-1, keepdims=True)

    def row_block(x, rows):
        return jnp.broadcast_to(x[rows], (COUNT_ROWS, LANES))

    tied = count_rows(lambda rows: (lambda keyb, col0, t=row_block(thr, rows): keyb >= t)) > float(topk)
    cut_ref[...] = jnp.full(cut_ref.shape, INT_MAX, jnp.int32)

    @pl.when(jnp.sum(jnp.where(tied, 1.0, 0.0)) > 0.0)
    def _():
        above = count_rows(lambda rows: (lambda keyb, col0, t=row_block(thr, rows): keyb > t))
        keep = float(topk) - above
        lane_col = lax.broadcasted_iota(jnp.int32, (COUNT_ROWS, LANES), 1)
        n_bits = (k_ref.shape[0] - 1).bit_length()

        def widen(it, cut):
            cand = cut + jnp.left_shift(jnp.int32(1), n_bits - 1 - it)

            def make_hit(rows):
                t, cb = row_block(thr, rows), row_block(cand, rows)
                return lambda keyb, col0: jnp.logical_and(keyb == t, col0 + lane_col < cb)

            return jnp.where(count_rows(make_hit) < keep, cand, cut)

        cut = lax.fori_loop(0, n_bits, widen, jnp.zeros((tq, 1), jnp.int32))
        cut_ref[...] = jnp.broadcast_to(jnp.where(tied, cut, INT_MAX), cut_ref.shape)

    cut = cut_ref[:, 0:1]

    def selection_bias(c):
        key = keys_ref[c]
        col = c * tk + lax.broadcasted_iota(jnp.int32, (tq, tk), 1)
        kept_tie = jnp.logical_and(key == thr, col <= cut)
        return jnp.where(jnp.logical_or(key > thr, kept_tie), 0.0, MASK_NEG)

    n_heads = q_ref.shape[1] // HEAD_DIM
    group = n_heads // N_KV_HEADS
    gr = group * tq
    n_sub = tk // LANES
    for h in range(n_heads):
        qs_ref[h * tq:(h + 1) * tq, :] = q_ref[:, h * HEAD_DIM:(h + 1) * HEAD_DIM]

    def masked_logits(c, bias, g):
        start = pl.multiple_of(c * tk, tk)
        logits = _dot_nt(qs_ref[g * gr:(g + 1) * gr, :], k_ref[pl.ds(start, tk), g * HEAD_DIM:(g + 1) * HEAD_DIM])
        return (logits.reshape(group, tq, tk) + bias[None]).reshape(gr, tk)

    m_ref[...] = jnp.full(m_ref.shape, MASK_NEG, F32)

    def max_chunk(c, carry):
        bias = selection_bias(c)
        for g in range(N_KV_HEADS):
            logits = masked_logits(c, bias, g)
            m_part = m_ref[g * gr:(g + 1) * gr, :]
            for u in range(n_sub):
                m_part = jnp.maximum(m_part, logits[:, u * LANES:(u + 1) * LANES])
            m_ref[g * gr:(g + 1) * gr, :] = m_part
        return carry

    lax.fori_loop(0, n_chunks, max_chunk, 0)
    for g in range(N_KV_HEADS):
        rows = slice(g * gr, (g + 1) * gr)
        m_ref[rows, :] = jnp.broadcast_to(jnp.max(m_ref[rows, :], axis=-1, keepdims=True), (gr, LANES))

    acc_ref[...] = jnp.zeros(acc_ref.shape, F32)

    def acc_chunk(c, carry):
        start = pl.multiple_of(c * tk, tk)
        bias = selection_bias(c)
        ones = jnp.ones((tk, HEAD_DIM), BF16)
        for g in range(N_KV_HEADS):
            rows = slice(g * gr, (g + 1) * gr)
            logits = masked_logits(c, bias, g)
            m_row = m_ref[rows, :]
            p = jnp.concatenate([jnp.exp(logits[:, u * LANES:(u + 1) * LANES] - m_row) for u in range(n_sub)],
                                axis=1).astype(BF16)
            v_ext = jnp.concatenate([v_ref[pl.ds(start, tk), g * HEAD_DIM:(g + 1) * HEAD_DIM], ones], axis=1)
            acc_ref[rows, :] += _dot(p, v_ext)
        return carry

    lax.fori_loop(0, n_chunks, acc_chunk, 0)
    for h in range(n_heads):
        rows = slice(h * tq, (h + 1) * tq)
        o_ref[:, h * HEAD_DIM:(h + 1) * HEAD_DIM] = (
            acc_ref[rows, :HEAD_DIM] / acc_ref[rows, HEAD_DIM:HEAD_DIM + 1]).astype(BF16)


def _attention(q, qi, w, k, proj, ki, lay, B, S):
    M, attn_w = q.shape
    kv_w = k.shape[1]
    idx_w = qi.shape[1]
    tq = min(512, S)
    tk = min(512, S)
    nq = S // tq
    topk = min(TOPK_MAX, S // 4)
    v_col = lay.off["v"] // kv_w
    return pl.pallas_call(
        functools.partial(_attn_kernel, tq=tq, tk=tk, topk=topk),
        grid=(B, nq),
        in_specs=[
            pl.BlockSpec((tq, attn_w), lambda b, i: (b * nq + i, 0)),
            pl.BlockSpec((tq, idx_w), lambda b, i: (b * nq + i, 0)),
            pl.BlockSpec((tq, LANES), lambda b, i: (b * nq + i, 0)),
            pl.BlockSpec((S, kv_w), lambda b, i: (b, 0), pipeline_mode=pl.Buffered(1)),
            pl.BlockSpec((S, kv_w), lambda b, i: (b, v_col), pipeline_mode=pl.Buffered(1)),
            pl.BlockSpec((S, LANES), lambda b, i: (b, 0), pipeline_mode=pl.Buffered(1)),
        ],
        out_specs=pl.BlockSpec((tq, attn_w), lambda b, i: (b * nq + i, 0)),
        out_shape=jax.ShapeDtypeStruct((M, attn_w), BF16),
        scratch_shapes=[pltpu.VMEM((S // tk, tq, tk), jnp.int32),
                        pltpu.VMEM((tq, LANES), jnp.int32),
                        pltpu.VMEM((attn_w // HEAD_DIM * tq, HEAD_DIM), BF16),
                        pltpu.VMEM((attn_w // HEAD_DIM * tq, LANES), F32),
                        pltpu.VMEM((attn_w // HEAD_DIM * tq, 2 * HEAD_DIM), F32)],
        compiler_params=_cparams(("parallel", "arbitrary"), 56),
        name="indexer_attention",
    )(q, qi, w, k, proj, ki)


def _branch_kernel(al_ref, aa_ref, ap_ref, g0_ref, g1_ref, g2_ref, wl_ref, wa_ref, wp_ref, o_ref):
    merged = (jax.nn.sigmoid(g0_ref[...].astype(F32)) * _dot(al_ref[...], wl_ref[...])
              + jax.nn.sigmoid(g1_ref[...].astype(F32)) * _dot(aa_ref[...], wa_ref[...])
              + jax.nn.sigmoid(g2_ref[...].astype(F32)) * _dot(ap_ref[...], wp_ref[...]))
    o_ref[...] = merged.astype(o_ref.dtype)


def _branch_merge(l, a_lru, a_attn, a_pool, proj, lay, w_l, w_a, w_p, S):
    M = a_lru.shape[0]
    D = w_l.shape[-1]
    tm = min(1024, S)
    tn = min(512, D)
    g_base = lay.off["g_br"] // tn
    nj = D // tn

    def gspec(br):
        return pl.BlockSpec((tm, tn), lambda i, j: (i, g_base + br * nj + j))

    def aspec(a):
        return pl.BlockSpec((tm, a.shape[1]), lambda i, j: (i, 0), pipeline_mode=pl.Buffered(1))

    def wspec(w):
        return pl.BlockSpec((None, w.shape[1], tn), lambda i, j: (l, 0, j))

    return pl.pallas_call(
        _branch_kernel,
        grid=(M // tm, nj),
        in_specs=[aspec(a_lru), aspec(a_attn), aspec(a_pool), gspec(0), gspec(1), gspec(2),
                  wspec(w_l), wspec(w_a), wspec(w_p)],
        out_specs=pl.BlockSpec((tm, tn), lambda i, j: (i, j)),
        out_shape=jax.ShapeDtypeStruct((M, D), BF16),
        compiler_params=_cparams(("parallel", "arbitrary"), 56),
        name="branch_merge",
    )(a_lru, a_attn, a_pool, proj, proj, proj, w_l, w_a, w_p)


def _outproj_kernel(x_ref, m_ref, gpost_ref, a_ref, w_ref, o_ref):
    j = pl.program_id(1)

    @pl.when(j == 0)
    def _():
        o_ref[...] = jnp.zeros_like(o_ref)

    o_ref[...] += _dot(a_ref[...], w_ref[...])

    @pl.when(j == pl.num_programs(1) - 1)
    def _():
        _postnorm_residual(x_ref, o_ref, m_ref, gpost_ref, 1, 1.0)


def _outproj(x, mods, l, gpost, merged, w_o, S):
    M, D = x.shape
    K = merged.shape[1]
    tm = min(512, S)
    tk = min(1024, K)
    tpb = S // tm
    return pl.pallas_call(
        _outproj_kernel,
        grid=(M // tm, K // tk),
        in_specs=[
            pl.BlockSpec((tm, D), lambda i, j: (i, 0), pipeline_mode=pl.Buffered(1)),
            pl.BlockSpec((None, None, N_MOD, D), lambda i, j: (l, i // tpb, 0, 0)),
            pl.BlockSpec((1, D), lambda i, j: (0, 0)),
            pl.BlockSpec((tm, tk), lambda i, j: (i, j)),
            pl.BlockSpec((None, tk, D), lambda i, j: (l, j, 0)),
        ],
        out_specs=pl.BlockSpec((tm, D), lambda i, j: (i, 0)),
        out_shape=jax.ShapeDtypeStruct((M, D), F32),
        compiler_params=_cparams(("parallel", "arbitrary"), 56),
        name="out_proj",
    )(x, mods, gpost.reshape(1, D), merged, w_o)


def kernel(x, c, positions, w_mod, b_mod, mod_offset, norm_pre, norm_post, ffn1_w_gate, ffn1_w_up, ffn1_w_down, w_in, conv_w, conv_b, lru_w_a, lru_b_a, lru_w_x, lru_b_x, lru_lambda, pool_w, pool_scale, w_br_lru, w_br_attn, w_br_pool, w_out, ffn2_w_gate, ffn2_w_up, ffn2_w_down):
    B, S, D = x.shape
    M = B * S
    depth = w_in.shape[0]
    lru_w = conv_w.shape[-1]
    attn_w = w_br_attn.shape[1]
    kv_w = N_KV_HEADS * HEAD_DIM
    idx_w = IDX_HEADS * IDX_DIM
    pool_wd = pool_scale.shape[-1]
    tn_proj = 768
    lay = _Layout(D, lru_w, attn_w, kv_w, idx_w, pool_wd, tn_proj)
    assert lay.d_in == w_in.shape[-1]
    assert B <= MOD_ROWS and S % LANES == 0

    bf = _cast_bf16
    w_in_p = lay.pack_t(w_in)
    tf = min(FFN_TF, ffn1_w_gate.shape[-1])
    f1gu, f1d = _cast_pair_bf16(ffn1_w_gate, ffn1_w_up, tf), bf(ffn1_w_down)
    f2gu, f2d = _cast_pair_bf16(ffn2_w_gate, ffn2_w_up, tf), bf(ffn2_w_down)
    wbl, wba, wbp, wo = bf(w_br_lru), bf(w_br_attn), bf(w_br_pool), bf(w_out)
    wa, wx, pw = lru_w_a.astype(BF16), lru_w_x.astype(BF16), pool_w.astype(BF16)

    mods = _modulation(c, w_mod, b_mod, mod_offset)
    tabs = _rope_tables(positions)

    xf = x.reshape(M, D)
    for l in range(depth):
        xf = _ffn(xf, mods, l, 0, norm_pre[l, 0], norm_post[l, 0], f1gu, f1d, S)
        proj = _proj(xf, mods, l, norm_pre[l, 1], w_in_p, S, tn_proj)
        q, k, qi, ki, w = _prep(proj, tabs, lay, attn_w, kv_w, idx_w, S)
        a_lru, a_pool = _seq(proj, lay, B, S, conv_w[l], conv_b[l], wa[l], lru_b_a[l], wx[l], lru_b_x[l],
                             lru_lambda[l], pw[l], pool_scale[l])
        a_attn = _attention(q, qi, w, k, proj, ki, lay, B, S)
        merged = _branch_merge(l, a_lru, a_attn, a_pool, proj, lay, wbl, wba, wbp, S)
        xf = _outproj(xf, mods, l, norm_post[l, 1], merged, wo, S)
        xf = _ffn(xf, mods, l, 2, norm_pre[l, 2], norm_post[l, 2], f2gu, f2d, S)
    return xf.reshape(B, S, D)
```
